```python
import jax, jax.numpy as jnp
from jax import lax
import numpy as np

D_MODEL = 1024
BATCH = 8
SEQ = 4096
DEPTH = 4

N_MIXERS = 3
EPS = 1e-6
D_FF = 2816
MLSTM_HEADS = 4
MLSTM_DH = D_MODEL // MLSTM_HEADS
MLSTM_CHUNK = 64
CONV_WIDTH = 4
MLSTM_F_BIAS = 3.0
FOX_HEADS = 8
FOX_DH = D_MODEL // FOX_HEADS
Q_BLOCK = 128
FOX_F_BIAS = 2.0
GLA_HEADS = 4
GLA_DK_TOTAL = D_MODEL // 2
GLA_DK = GLA_DK_TOTAL // GLA_HEADS
GLA_DV = D_MODEL // GLA_HEADS
GLA_RANK = 16
GLA_TAU = 16.0
GLA_CHUNK = 64
N_A = (DEPTH + 2) // 3
N_B = (DEPTH + 1) // 3
N_C = DEPTH // 3

kernel_name = "hybrid_mlstm_fox_gla_macaron"


def rmsnorm(x, w):
    xf = x.astype(jnp.float32)
    y = xf * lax.rsqrt(jnp.mean(xf * xf, axis=-1, keepdims=True) + EPS)
    return (y * w.astype(jnp.float32)).astype(x.dtype)


def swiglu(x, w_in, w_out):
    g, u = jnp.split(x @ w_in, 2, axis=-1)
    return (jax.nn.silu(g) * u) @ w_out


def split_heads(t, n_heads):
    b, s, w = t.shape
    return t.reshape(b, s, n_heads, w // n_heads).transpose(0, 2, 1, 3)


def to_chunks(t, size):
    b, h, s = t.shape[:3]
    t = t.reshape(b, h, s // size, size, *t.shape[3:])
    return jnp.moveaxis(t, 2, 0)


def from_chunks(t):
    nc, b, h, l, d = t.shape
    return jnp.moveaxis(t, 0, 2).reshape(b, h, nc * l, d)


def causal_conv(x, w):
    k, c = w.shape
    return lax.conv_general_dilated(x, w[:, None, :], window_strides=(1,), padding=[(k - 1, 0)],
                                    dimension_numbers=('NWC', 'WIO', 'NWC'), feature_group_count=c)


def mlstm_chunkwise(q, k, v, log_i, log_f):
    b, h, s, dh = q.shape
    L = MLSTM_CHUNK
    causal = jnp.tril(jnp.ones((L, L), dtype=bool))

    def step(carry, inp):
        C, n, m = carry
        qc, kc, vc, ic, fc = inp
        bcum = jnp.cumsum(fc, axis=-1)
        log_d = jnp.where(causal, bcum[..., :, None] - bcum[..., None, :] + ic[..., None, :], -jnp.inf)
        log_inter = bcum + m[..., None]
        m_t = jnp.maximum(log_inter, jnp.max(log_d, axis=-1))
        d = jnp.exp(log_d - m_t[..., None])
        inter = jnp.exp(log_inter - m_t)
        sc = jnp.einsum('bhtd,bhsd->bhts', qc, kc) * d
        num = jnp.einsum('bhts,bhse->bhte', sc, vc) + inter[..., None] * jnp.einsum('bhtd,bhde->bhte', qc, C)
        den = jnp.sum(sc, axis=-1) + inter * jnp.einsum('bhtd,bhd->bht', qc, n)
        h_out = num / jnp.maximum(jnp.abs(den), jnp.exp(-m_t))[..., None]
        b_last = bcum[..., -1]
        log_w = b_last[..., None] - bcum + ic
        m_new = jnp.maximum(b_last + m, jnp.max(log_w, axis=-1))
        w = jnp.exp(log_w - m_new[..., None])
        decay = jnp.exp(b_last + m - m_new)
        C = decay[..., None, None] * C + jnp.einsum('bhs,bhsd,bhse->bhde', w, kc, vc)
        n = decay[..., None] * n + jnp.einsum('bhs,bhsd->bhd', w, kc)
        return (C, n, m_new), h_out

    init = (jnp.zeros((b, h, dh, dh), jnp.float32), jnp.zeros((b, h, dh), jnp.float32),
            jnp.zeros((b, h), jnp.float32))
    xs = (to_chunks(q, L), to_chunks(k, L), to_chunks(v, L), to_chunks(log_i, L), to_chunks(log_f, L))
    _, hs = lax.scan(step, init, xs)
    return from_chunks(hs)


def mlstm_mixer(x, w_in, b_gate, conv_w, norm_w, w_out):
    b, s, _ = x.shape
    H, D = MLSTM_HEADS, D_MODEL
    proj = x @ w_in
    qk, v, o, gates = jnp.split(proj, [2 * D, 3 * D, 4 * D], axis=-1)
    qk = jax.nn.silu(causal_conv(qk, conv_w))
    q, k = jnp.split(qk, 2, axis=-1)
    gates = gates.astype(jnp.float32) + b_gate.astype(jnp.float32)
    log_i = gates[..., :H].transpose(0, 2, 1)
    log_f = jax.nn.log_sigmoid(gates[..., H:]).transpose(0, 2, 1)
    q = split_heads(q, H).astype(jnp.float32) * MLSTM_DH ** -0.5
    k = split_heads(k, H).astype(jnp.float32)
    v = split_heads(v, H).astype(jnp.float32)
    hs = mlstm_chunkwise(q, k, v, log_i, log_f).astype(x.dtype)
    hs = rmsnorm(hs.transpose(0, 2, 1, 3), norm_w.reshape(H, MLSTM_DH)).reshape(b, s, D)
    return (jax.nn.sigmoid(o) * hs) @ w_out


def fox_attention(q, k, v, F):
    b, h, s, dh = q.shape
    nb = s // Q_BLOCK
    qb = jnp.moveaxis(q.reshape(b, h, nb, Q_BLOCK, dh), 2, 0)
    Fb = jnp.moveaxis(F.reshape(b, h, nb, Q_BLOCK), 2, 0)
    kpos = jnp.arange(s)
    scale = dh ** -0.5

    def block(args):
        qi, Fi, i = args
        qpos = i * Q_BLOCK + jnp.arange(Q_BLOCK)
        logits = jnp.einsum('bhqd,bhkd->bhqk', qi, k, preferred_element_type=jnp.float32) * scale
        logits = logits + Fi[..., :, None] - F[..., None, :]
        logits = jnp.where(kpos[None, :] <= qpos[:, None], logits, -jnp.inf)
        p = jax.nn.softmax(logits, axis=-1)
        return jnp.einsum('bhqk,bhkd->bhqd', p.astype(v.dtype), v)

    out = lax.map(block, (qb, Fb, jnp.arange(nb)))
    return from_chunks(out)


def fox_mixer(x, w_in, b_f, qk_norm, w_out):
    b, s, _ = x.shape
    H, D = FOX_HEADS, D_MODEL
    proj = x @ w_in
    q, k, v, o, fg = jnp.split(proj, [D, 2 * D, 3 * D, 4 * D], axis=-1)
    log_f = jax.nn.log_sigmoid(fg.astype(jnp.float32) + b_f.astype(jnp.float32))
    F = jnp.cumsum(log_f, axis=1).transpose(0, 2, 1)
    q = rmsnorm(split_heads(q, H), qk_norm[0])
    k = rmsnorm(split_heads(k, H), qk_norm[1])
    v = split_heads(v, H)
    att = fox_attention(q, k, v, F).transpose(0, 2, 1, 3).reshape(b, s, D)
    return (att * jax.nn.sigmoid(o)) @ w_out


def gla_chunked(q, k, v, log_a):
    b, h, s, dk = q.shape
    dv = v.shape[-1]
    L = GLA_CHUNK
    causal = jnp.tril(jnp.ones((L, L), dtype=bool))

    def step(S_prev, inp):
        qc, kc, vc, gc = inp
        g = jnp.cumsum(gc, axis=2)
        rel = jnp.where(causal[:, :, None], g[:, :, :, None, :] - g[:, :, None, :, :], -jnp.inf)
        A = jnp.einsum('bhtd,bhsd,bhtsd->bhts', qc, kc, jnp.exp(rel))
        o = jnp.einsum('bhts,bhse->bhte', A, vc) + jnp.einsum('bhtd,bhde->bhte', qc * jnp.exp(g), S_prev)
        g_last = g[:, :, -1]
        S_new = jnp.exp(g_last)[..., None] * S_prev + jnp.einsum(
            'bhsd,bhse->bhde', kc * jnp.exp(g_last[:, :, None, :] - g), vc)
        return S_new, o

    init = jnp.zeros((b, h, dk, dv), jnp.float32)
    xs = (to_chunks(q, L), to_chunks(k, L), to_chunks(v, L), to_chunks(log_a, L))
    _, os_ = lax.scan(step, init, xs)
    return from_chunks(os_)


def gla_mixer(x, w_in, w_gate_up, b_gate, norm_w, w_out):
    b, s, _ = x.shape
    H, D, DKT = GLA_HEADS, D_MODEL, GLA_DK_TOTAL
    proj = x @ w_in
    q, k, v, r, glr = jnp.split(proj, [DKT, 2 * DKT, 2 * DKT + D, 2 * DKT + 2 * D], axis=-1)
    log_a = jax.nn.log_sigmoid((glr @ w_gate_up).astype(jnp.float32) + b_gate.astype(jnp.float32)) / GLA_TAU
    q = split_heads(q, H).astype(jnp.float32) * GLA_DK ** -0.5
    k = split_heads(k, H).astype(jnp.float32)
    v = split_heads(v, H).astype(jnp.float32)
    log_a = split_heads(log_a, H)
    o = gla_chunked(q, k, v, log_a).astype(x.dtype)
    o = rmsnorm(o.transpose(0, 2, 1, 3), norm_w.reshape(H, GLA_DV)).reshape(b, s, D)
    return (o * jax.nn.silu(r)) @ w_out


def setup_inputs(seed: int = 0) -> dict:
    key = jax.random.key(seed)
    ks = jax.random.split(key, 24)
    D, F = D_MODEL, D_FF
    nrm = lambda k, shape, fan_in: jax.random.normal(k, shape, jnp.float32) * fan_in ** -0.5
    gain = lambda k, shape: 1.0 + 0.02 * jax.random.normal(k, shape, jnp.float32)
    small = lambda k, shape: 0.01 * jax.random.normal(k, shape, jnp.float32)
    mlstm_b = jnp.concatenate([small(ks[6], (N_A, MLSTM_HEADS)),
                               MLSTM_F_BIAS + 0.1 * jax.random.normal(ks[7], (N_A, MLSTM_HEADS), jnp.float32)], axis=-1)
    return {
        "x": jax.random.normal(ks[0], (BATCH, SEQ, D), jnp.float32),
        "norm_w": gain(ks[1], (DEPTH, 3, D)),
        "ffn_w_in": nrm(ks[2], (DEPTH, 2, D, 2 * F), D),
        "ffn_w_out": nrm(ks[3], (DEPTH, 2, F, D), F),
        "mlstm_w_in": nrm(ks[4], (N_A, D, 4 * D + 2 * MLSTM_HEADS), D),
        "mlstm_b_gate": mlstm_b,
        "mlstm_conv_w": nrm(ks[8], (N_A, CONV_WIDTH, 2 * D), CONV_WIDTH),
        "mlstm_norm_w": gain(ks[9], (N_A, D)),
        "mlstm_w_out": nrm(ks[10], (N_A, D, D), D),
        "fox_w_in": nrm(ks[11], (N_B, D, 4 * D + FOX_HEADS), D),
        "fox_b_f": FOX_F_BIAS + 0.1 * jax.random.normal(ks[12], (N_B, FOX_HEADS), jnp.float32),
        "fox_qk_norm": gain(ks[13], (N_B, 2, FOX_DH)),
        "fox_w_out": nrm(ks[14], (N_B, D, D), D),
        "gla_w_in": nrm(ks[15], (N_C, D, 2 * GLA_DK_TOTAL + 2 * D + GLA_RANK), D),
        "gla_w_gate_up": nrm(ks[16], (N_C, GLA_RANK, GLA_DK_TOTAL), GLA_RANK),
        "gla_b_gate": small(ks[17], (N_C, GLA_DK_TOTAL)),
        "gla_norm_w": gain(ks[18], (N_C, D)),
        "gla_w_out": nrm(ks[19], (N_C, D, D), D),
        "final_norm_w": gain(ks[20], (D,)),
    }


def reference(x, norm_w, ffn_w_in, ffn_w_out, mlstm_w_in, mlstm_b_gate, mlstm_conv_w, mlstm_norm_w,
              mlstm_w_out, fox_w_in, fox_b_f, fox_qk_norm, fox_w_out, gla_w_in, gla_w_gate_up, gla_b_gate,
              gla_norm_w, gla_w_out, final_norm_w):
    for layer in range(DEPTH):
        kind, j = layer % N_MIXERS, layer // N_MIXERS
        x = x + 0.5 * swiglu(rmsnorm(x, norm_w[layer, 0]), ffn_w_in[layer, 0], ffn_w_out[layer, 0])
        xn = rmsnorm(x, norm_w[layer, 1])
        if kind == 0:
            mix = mlstm_mixer(xn, mlstm_w_in[j], mlstm_b_gate[j], mlstm_conv_w[j], mlstm_norm_w[j], mlstm_w_out[j])
        elif kind == 1:
            mix = fox_mixer(xn, fox_w_in[j], fox_b_f[j], fox_qk_norm[j], fox_w_out[j])
        else:
            mix = gla_mixer(xn, gla_w_in[j], gla_w_gate_up[j], gla_b_gate[j], gla_norm_w[j], gla_w_out[j])
        x = x + mix
        x = x + 0.5 * swiglu(rmsnorm(x, norm_w[layer, 2]), ffn_w_in[layer, 1], ffn_w_out[layer, 1])
    return rmsnorm(x, final_norm_w)
```

```python
import functools

import numpy as np
import jax
import jax.numpy as jnp
from jax import lax
from jax.experimental import pallas as pl
from jax.experimental.pallas import tpu as pltpu

F32 = jnp.float32
BF16 = jnp.bfloat16

EPS = 1e-6
D_MODEL = 1024
D_FF = 2816
MLSTM_HEADS = 4
MLSTM_DH = D_MODEL // MLSTM_HEADS
CONV_WIDTH = 4
FOX_HEADS = 8
FOX_DH = D_MODEL // FOX_HEADS
GLA_HEADS = 4
GLA_DK_TOTAL = D_MODEL // 2
GLA_DK = GLA_DK_TOTAL // GLA_HEADS
GLA_DV = D_MODEL // GLA_HEADS
GLA_RANK = 16
GLA_TAU = 16.0
N_MIXERS = 3

LANES = 128
SUBLANES = 8
VMEM_LIMIT_BYTES = 56 * 1024 * 1024

ROW_TILE = 512
MLSTM_CHUNK = 256
GLA_CHUNK = 128
GLA_LEVELS = 7
ATTN_TILE = 512


def _dot(a, b):
    return jnp.dot(a, b, preferred_element_type=F32)


def _dot_nt(a, b):
    return lax.dot_general(a, b, (((1,), (1,)), ((), ())), preferred_element_type=F32)


def _dot_tn(a, b):
    return lax.dot_general(a, b, (((0,), (0,)), ((), ())), preferred_element_type=F32)


def _rms(x, w):
    ms = jnp.mean(x * x, axis=-1, keepdims=True)
    return x * lax.rsqrt(ms + EPS) * w


def _sigmoid(x):
    return 1.0 / (1.0 + jnp.exp(-x))


def _log_sigmoid(x):
    return jnp.minimum(x, 0.0) - jnp.log1p(jnp.exp(-jnp.abs(x)))


def _split3(x):
    hi = x.astype(BF16)
    r1 = x - hi.astype(F32)
    mid = r1.astype(BF16)
    lo = (r1 - mid.astype(F32)).astype(BF16)
    return hi, mid, lo


def _cumsum_rows(tril, x):
    hi, mid, lo = _split3(x)
    return _dot(tril, hi) + _dot(tril, mid) + _dot(tril, lo)


def _cumsum_lanes(x, triu):
    hi, mid, lo = _split3(x)
    return _dot(hi, triu) + _dot(mid, triu) + _dot(lo, triu)


def _tri(n):
    row = lax.broadcasted_iota(jnp.int32, (n, n), 0)
    col = lax.broadcasted_iota(jnp.int32, (n, n), 1)
    tril = jnp.where(row >= col, 1.0, 0.0).astype(BF16)
    triu = jnp.where(row <= col, 1.0, 0.0).astype(BF16)
    return row, col, tril, triu


def _resident(shape):
    zeros = (0,) * len(shape)
    return pl.BlockSpec(shape, lambda *_: zeros, pipeline_mode=pl.Buffered(1))


def _params(semantics):
    return pltpu.CompilerParams(dimension_semantics=semantics, vmem_limit_bytes=VMEM_LIMIT_BYTES)


def _row_tile(s):
    return min(ROW_TILE, s)


def _ffn_kernel(x_ref, nw_ref, win_ref, wout_ref, fw_ref, o_ref, *, final):
    x = x_ref[...]
    xn = _rms(x, nw_ref[...]).astype(BF16)
    g = _dot(xn, win_ref[:, :D_FF])
    u = _dot(xn, win_ref[:, D_FF:])
    a = (g * _sigmoid(g) * u).astype(BF16)
    y = x + 0.5 * _dot(a, wout_ref[...])
    if final:
        y = _rms(y, fw_ref[...])
    o_ref[...] = y


def _ffn(x, nw, w_in, w_out, fw, final):
    m = x.shape[0]
    tm = _row_tile(m)
    row = pl.BlockSpec((tm, D_MODEL), lambda i: (i, 0))
    return pl.pallas_call(
        functools.partial(_ffn_kernel, final=final),
        grid=(m // tm,),
        in_specs=[row, _resident((1, D_MODEL)), _resident((D_MODEL, 2 * D_FF)),
                  _resident((D_FF, D_MODEL)), _resident((1, D_MODEL))],
        out_specs=row,
        out_shape=jax.ShapeDtypeStruct((m, D_MODEL), F32),
        compiler_params=_params(("parallel",)),
        name="ffn_final" if final else "ffn",
    )(x, nw, w_in, w_out, fw)


def _outproj_kernel(x_ref, a_ref, w_ref, o_ref):
    o_ref[...] = x_ref[...] + _dot(a_ref[...], w_ref[...])


def _outproj(x, a, w):
    m = x.shape[0]
    tm = _row_tile(m)
    row = pl.BlockSpec((tm, D_MODEL), lambda i: (i, 0))
    return pl.pallas_call(
        _outproj_kernel,
        grid=(m // tm,),
        in_specs=[row, row, _resident((D_MODEL, D_MODEL))],
        out_specs=row,
        out_shape=jax.ShapeDtypeStruct((m, D_MODEL), F32),
        compiler_params=_params(("parallel",)),
        name="outproj",
    )(x, a, w)


def _mlstm_proj_kernel(x_ref, nw_ref, wqk_ref, wv_ref, wo_ref, wg_ref, wgt_ref, bcol_ref, brow_ref, cw_ref,
                       q_ref, k_ref, v_ref, so_ref, gcol_ref, grow_ref, carry_ref):
    tm = x_ref.shape[0]

    @pl.when(pl.program_id(1) == 0)
    def _():
        carry_ref[...] = jnp.zeros_like(carry_ref)

    xn = _rms(x_ref[...], nw_ref[...]).astype(BF16)
    pre = _dot(xn, wqk_ref[...])
    ext = jnp.concatenate([carry_ref[...], pre], axis=0)
    cw = cw_ref[...]
    conv = pre * cw[CONV_WIDTH - 1:CONV_WIDTH]
    for j in range(1, CONV_WIDTH):
        shifted = pltpu.roll(ext, j, 0)[SUBLANES:]
        conv = conv + shifted * cw[CONV_WIDTH - 1 - j:CONV_WIDTH - j]
    carry_ref[...] = pre[tm - SUBLANES:]
    act = conv * _sigmoid(conv)
    q_ref[...] = (act[:, :D_MODEL] * MLSTM_DH ** -0.5).astype(BF16)
    k_ref[...] = act[:, D_MODEL:].astype(BF16)
    v_ref[...] = _dot(xn, wv_ref[...]).astype(BF16)
    so_ref[...] = _sigmoid(_dot(xn, wo_ref[...])).astype(BF16)
    gcol_ref[...] = _dot(xn, wg_ref[...]) + bcol_ref[...]
    grow_ref[...] = _dot_nt(wgt_ref[...], xn) + brow_ref[...]


def _mlstm_core_kernel(q_ref, k_ref, v_ref, so_ref, gcol_ref, grow_ref, nw_ref, o_ref, c_ref, n_ref, m_ref):
    L = q_ref.shape[0]
    H, DH = MLSTM_HEADS, MLSTM_DH

    @pl.when(pl.program_id(1) == 0)
    def _():
        c_ref[...] = jnp.zeros_like(c_ref)
        n_ref[...] = jnp.zeros_like(n_ref)
        m_ref[...] = jnp.zeros_like(m_ref)

    row, col, tril, triu = _tri(L)
    causal = row >= col
    gcol = gcol_ref[...]
    grow = grow_ref[...]
    bc_all = _cumsum_rows(tril, _log_sigmoid(gcol))
    br_all = _cumsum_lanes(_log_sigmoid(grow), triu)

    for h in range(H):
        hs = slice(h * DH, (h + 1) * DH)
        q = q_ref[:, hs]
        k = k_ref[:, hs]
        v = v_ref[:, hs]
        bc = bc_all[:, H + h:H + h + 1]
        br = br_all[H + h:H + h + 1, :]
        ic = gcol[:, h:h + 1]
        ir = grow[h:h + 1, :]
        m_prev = m_ref[h][:, 0:1]
        c_prev = c_ref[h]
        n_prev = n_ref[h]

        log_d = jnp.where(causal, bc - br + ir, -jnp.inf)
        log_inter = bc + m_prev
        m_t = jnp.maximum(log_inter, jnp.max(log_d, axis=-1, keepdims=True))
        d = jnp.exp(log_d - m_t)
        inter = jnp.exp(log_inter - m_t)
        sc = _dot_nt(q, k) * d
        num = _dot(sc.astype(BF16), v) + inter * _dot(q, c_prev.astype(BF16))
        qn = jnp.sum(q.astype(F32) * n_prev, axis=-1, keepdims=True)
        den = jnp.sum(sc, axis=-1, keepdims=True) + inter * qn
        h_out = num * (1.0 / jnp.maximum(jnp.abs(den), jnp.exp(-m_t)))

        b_last = bc[L - 1:L, :]
        log_w_col = b_last - bc + ic
        log_w_row = b_last - br + ir
        m_new = jnp.maximum(b_last + m_prev, jnp.max(log_w_row, axis=-1, keepdims=True))
        decay = jnp.exp(b_last + m_prev - m_new)
        kw = k.astype(F32) * jnp.exp(log_w_col - m_new)
        c_ref[h] = decay * c_prev + _dot_tn(kw.astype(BF16), v)
        n_ref[h] = decay * n_prev + jnp.sum(kw, axis=0, keepdims=True)
        m_ref[h] = jnp.broadcast_to(m_new, (1, LANES))

        hn = _rms(h_out, nw_ref[:, hs])
        o_ref[:, hs] = (hn * so_ref[:, hs].astype(F32)).astype(BF16)


def _mlstm_mixer(x, b, s, nw, w_in, b_gate, conv_w, norm_w, w_out):
    m = b * s
    D, H = D_MODEL, MLSTM_HEADS
    tm = _row_tile(s)
    nt = s // tm
    w = w_in.astype(BF16)
    wqk, wv, wo = w[:, :2 * D], w[:, 2 * D:3 * D], w[:, 3 * D:4 * D]
    wg = jnp.pad(w[:, 4 * D:], ((0, 0), (0, LANES - 2 * H)))
    wgt = w[:, 4 * D:].T
    bcol = jnp.pad(b_gate, (0, LANES - 2 * H)).reshape(1, LANES)
    brow = b_gate.reshape(2 * H, 1)

    row = pl.BlockSpec((tm, D), lambda i, j: (i * nt + j, 0))
    q, k, v, so, gcol, grow = pl.pallas_call(
        _mlstm_proj_kernel,
        grid=(b, nt),
        in_specs=[row, _resident((1, D)), _resident((D, 2 * D)), _resident((D, D)), _resident((D, D)),
                  _resident((D, LANES)), _resident((2 * H, D)), _resident((1, LANES)), _resident((2 * H, 1)),
                  _resident((CONV_WIDTH, 2 * D))],
        out_specs=[row, row, row, row,
                   pl.BlockSpec((tm, LANES), lambda i, j: (i * nt + j, 0)),
                   pl.BlockSpec((2 * H, tm), lambda i, j: (0, i * nt + j))],
        out_shape=[jax.ShapeDtypeStruct((m, D), BF16)] * 4 + [
            jax.ShapeDtypeStruct((m, LANES), F32), jax.ShapeDtypeStruct((2 * H, m), F32)],
        scratch_shapes=[pltpu.VMEM((SUBLANES, 2 * D), F32)],
        compiler_params=_params(("parallel", "arbitrary")),
        name="mlstm_proj",
    )(x, nw, wqk, wv, wo, wg, wgt, bcol, brow, conv_w)

    L = min(MLSTM_CHUNK, s)
    nc = s // L
    blk = pl.BlockSpec((L, D), lambda i, j: (i * nc + j, 0))
    hg = pl.pallas_call(
        _mlstm_core_kernel,
        grid=(b, nc),
        in_specs=[blk, blk, blk, blk,
                  pl.BlockSpec((L, LANES), lambda i, j: (i * nc + j, 0)),
                  pl.BlockSpec((2 * H, L), lambda i, j: (0, i * nc + j)),
                  _resident((1, D))],
        out_specs=blk,
        out_shape=jax.ShapeDtypeStruct((m, D), BF16),
        scratch_shapes=[pltpu.VMEM((H, MLSTM_DH, MLSTM_DH), F32), pltpu.VMEM((H, 1, MLSTM_DH), F32),
                        pltpu.VMEM((H, 1, LANES), F32)],
        compiler_params=_params(("parallel", "arbitrary")),
        name="mlstm_core",
    )(q, k, v, so, gcol, grow, norm_w.reshape(1, D))
    return _outproj(x, hg, w_out.astype(BF16))


def _fox_proj_kernel(x_ref, nw_ref, wq_ref, wk_ref, wv_ref, wo_ref, wf_ref, wft_ref, bcol_ref, brow_ref, qkn_ref,
                     q_ref, k_ref, v_ref, so_ref, fcol_ref, frow_ref, ccol_ref, crow_ref):
    tm = x_ref.shape[0]

    @pl.when(pl.program_id(1) == 0)
    def _():
        ccol_ref[...] = jnp.zeros_like(ccol_ref)
        crow_ref[...] = jnp.zeros_like(crow_ref)

    xn = _rms(x_ref[...], nw_ref[...]).astype(BF16)
    q = _dot(xn, wq_ref[...])
    k = _dot(xn, wk_ref[...])
    qkn = qkn_ref[...]
    for h in range(FOX_HEADS):
        hs = slice(h * FOX_DH, (h + 1) * FOX_DH)
        q_ref[:, hs] = (_rms(q[:, hs], qkn[0:1]) * FOX_DH ** -0.5).astype(BF16)
        k_ref[:, hs] = _rms(k[:, hs], qkn[1:2]).astype(BF16)
    v_ref[...] = _dot(xn, wv_ref[...]).astype(BF16)
    so_ref[...] = _sigmoid(_dot(xn, wo_ref[...])).astype(BF16)

    _, _, tril, triu = _tri(tm)
    lf_col = _log_sigmoid(_dot(xn, wf_ref[...]) + bcol_ref[...])
    lf_row = _log_sigmoid(_dot_nt(wft_ref[...], xn) + brow_ref[...])
    f_col = _cumsum_rows(tril, lf_col) + ccol_ref[...]
    f_row = _cumsum_lanes(lf_row, triu) + crow_ref[...]
    fcol_ref[...] = f_col
    for h in range(FOX_HEADS):
        frow_ref[h] = f_row[h:h + 1, :]
    ccol_ref[...] = f_col[tm - 1:tm, :]
    crow_ref[...] = f_row[:, tm - 1:tm]


def _fox_attn_kernel(qi_tab, ki_tab, q_ref, k_ref, v_ref, so_ref, fcol_ref, frow_ref, o_ref,
                     m_ref, l_ref, acc_ref, fq_ref):
    tq = q_ref.shape[0]
    tk = k_ref.shape[0]
    h = pl.program_id(1)
    t = pl.program_id(2)
    qi = qi_tab[t]
    ki = ki_tab[t]

    @pl.when(ki == 0)
    def _():
        m_ref[...] = jnp.full_like(m_ref, -jnp.inf)
        l_ref[...] = jnp.zeros_like(l_ref)
        acc_ref[...] = jnp.zeros_like(acc_ref)
        lane = lax.broadcasted_iota(jnp.int32, (tq, LANES), 1)
        fq_ref[...] = jnp.sum(jnp.where(lane == h, fcol_ref[...], 0.0), axis=-1, keepdims=True)

    def step(diagonal):
        s = _dot_nt(q_ref[...], k_ref[...]) + (fq_ref[...] - frow_ref[...])
        if diagonal:
            row = lax.broadcasted_iota(jnp.int32, (tq, tk), 0)
            col = lax.broadcasted_iota(jnp.int32, (tq, tk), 1)
            s = jnp.where(row >= col, s, -jnp.inf)
        m_prev = m_ref[...]
        m_new = jnp.maximum(m_prev, jnp.max(s, axis=-1, keepdims=True))
        alpha = jnp.exp(m_prev - m_new)
        p = jnp.exp(s - m_new)
        l_ref[...] = alpha * l_ref[...] + jnp.sum(p, axis=-1, keepdims=True)
        acc_ref[...] = alpha * acc_ref[...] + _dot(p.astype(BF16), v_ref[...])
        m_ref[...] = m_new

    @pl.when(ki < qi)
    def _():
        step(False)

    @pl.when(ki == qi)
    def _():
        step(True)
        att = acc_ref[...] * (1.0 / l_ref[...])
        o_ref[...] = (att * so_ref[...].astype(F32)).astype(BF16)


def _fox_mixer(x, b, s, nw, w_in, b_f, qk_norm, w_out):
    m = b * s
    D, H, DH = D_MODEL, FOX_HEADS, FOX_DH
    tm = _row_tile(s)
    nt = s // tm
    w = w_in.astype(BF16)
    wq, wk, wv, wo = w[:, :D], w[:, D:2 * D], w[:, 2 * D:3 * D], w[:, 3 * D:4 * D]
    wf = jnp.pad(w[:, 4 * D:], ((0, 0), (0, LANES - H)))
    wft = w[:, 4 * D:].T
    bcol = jnp.pad(b_f, (0, LANES - H)).reshape(1, LANES)
    brow = b_f.reshape(H, 1)

    row = pl.BlockSpec((tm, D), lambda i, j: (i * nt + j, 0))
    q, k, v, so, fcol, frow = pl.pallas_call(
        _fox_proj_kernel,
        grid=(b, nt),
        in_specs=[row, _resident((1, D)), _resident((D, D)), _resident((D, D)), _resident((D, D)),
                  _resident((D, D)), _resident((D, LANES)), _resident((H, D)), _resident((1, LANES)),
                  _resident((H, 1)), _resident((2, DH))],
        out_specs=[row, row, row, row,
                   pl.BlockSpec((tm, LANES), lambda i, j: (i * nt + j, 0)),
                   pl.BlockSpec((H, 1, tm), lambda i, j: (0, 0, i * nt + j))],
        out_shape=[jax.ShapeDtypeStruct((m, D), BF16)] * 4 + [
            jax.ShapeDtypeStruct((m, LANES), F32), jax.ShapeDtypeStruct((H, 1, m), F32)],
        scratch_shapes=[pltpu.VMEM((1, LANES), F32), pltpu.VMEM((H, 1), F32)],
        compiler_params=_params(("parallel", "arbitrary")),
        name="fox_proj",
    )(x, nw, wq, wk, wv, wo, wf, wft, bcol, brow, qk_norm)

    t = min(ATTN_TILE, s)
    nq = s // t
    pairs = [(i, j) for i in range(nq) for j in range(i + 1)]
    qi_tab = jnp.asarray(np.array([p[0] for p in pairs], np.int32))
    ki_tab = jnp.asarray(np.array([p[1] for p in pairs], np.int32))
    qblk = pl.BlockSpec((t, DH), lambda bi, hi, ti, qt, kt: (bi * nq + qt[ti], hi))
    kblk = pl.BlockSpec((t, DH), lambda bi, hi, ti, qt, kt: (bi * nq + kt[ti], hi))
    att = pl.pallas_call(
        _fox_attn_kernel,
        grid_spec=pltpu.PrefetchScalarGridSpec(
            num_scalar_prefetch=2,
            grid=(b, H, len(pairs)),
            in_specs=[qblk, kblk, kblk, qblk,
                      pl.BlockSpec((t, LANES), lambda bi, hi, ti, qt, kt: (bi * nq + qt[ti], 0)),
                      pl.BlockSpec((None, 1, t), lambda bi, hi, ti, qt, kt: (hi, 0, bi * nq + kt[ti]))],
            out_specs=qblk,
            scratch_shapes=[pltpu.VMEM((t, 1), F32), pltpu.VMEM((t, 1), F32), pltpu.VMEM((t, DH), F32),
                            pltpu.VMEM((t, 1), F32)],
        ),
        out_shape=jax.ShapeDtypeStruct((m, D), BF16),
        compiler_params=_params(("parallel", "parallel", "arbitrary")),
        name="fox_attn",
    )(qi_tab, ki_tab, q, k, v, so, fcol, frow)
    return _outproj(x, att, w_out.astype(BF16))


def _gla_level_matrix(L, levels):
    t = np.arange(L)[:, None]
    c = np.arange(L)[None, :]
    blocks = [(c <= t)]
    for j in range(levels):
        hs = 1 << j
        mid = (t & ~(2 * hs - 1)) + hs
        upper = (t & hs) != 0
        blocks.append(np.where(upper, (c > mid) & (c <= t), (c > t) & (c <= mid)))
    return np.concatenate(blocks, axis=0).astype(np.float32)


def _gla_proj_kernel(x_ref, nw_ref, wq_ref, wk_ref, wv_ref, wr_ref, wg_ref, wup_ref, bg_ref,
                     q_ref, k_ref, v_ref, sr_ref, la_ref):
    xn = _rms(x_ref[...], nw_ref[...]).astype(BF16)
    q_ref[...] = (_dot(xn, wq_ref[...]) * GLA_DK ** -0.5).astype(BF16)
    k_ref[...] = _dot(xn, wk_ref[...]).astype(BF16)
    v_ref[...] = _dot(xn, wv_ref[...]).astype(BF16)
    r = _dot(xn, wr_ref[...])
    sr_ref[...] = (r * _sigmoid(r)).astype(BF16)
    glr = _dot(xn, wg_ref[...])
    z = _dot(glr.astype(BF16), wup_ref[...]) + bg_ref[...]
    la_ref[...] = _log_sigmoid(z) * (1.0 / GLA_TAU)


def _gla_core_kernel(q_ref, k_ref, v_ref, sr_ref, la_ref, lvl_ref, nw_ref, o_ref, st_ref):
    L = q_ref.shape[0]
    H, DK, DV = GLA_HEADS, GLA_DK, GLA_DV
    levels = lvl_ref.shape[0] // L - 1

    @pl.when(pl.program_id(1) == 0)
    def _():
        st_ref[...] = jnp.zeros_like(st_ref)

    hi, mid, lo = _split3(la_ref[...])
    lvl = lvl_ref[...]
    args = _dot(lvl, hi) + _dot(lvl, mid) + _dot(lvl, lo)
    row = lax.broadcasted_iota(jnp.int32, (L, L), 0)
    col = lax.broadcasted_iota(jnp.int32, (L, L), 1)
    diff = jnp.where(row > col, row ^ col, 0)

    for h in range(H):
        ks = slice(h * DK, (h + 1) * DK)
        vs = slice(h * DV, (h + 1) * DV)
        q = q_ref[:, ks].astype(F32)
        k = k_ref[:, ks].astype(F32)
        v = v_ref[:, vs]
        g = args[0:L, ks]
        a = jnp.where(row == col, jnp.sum(q * k, axis=-1, keepdims=True), 0.0)
        for j in range(levels):
            e = jnp.exp(args[(j + 1) * L:(j + 2) * L, ks])
            aj = _dot_nt((q * e).astype(BF16), (k * e).astype(BF16))
            a = jnp.where((diff >> j) == 1, aj, a)
        st = st_ref[h]
        o = _dot(a.astype(BF16), v) + _dot_nt((q * jnp.exp(g)).astype(BF16), st.astype(BF16))
        g_last = g[L - 1:L, :]
        kd = (k * jnp.exp(g_last - g)).astype(BF16)
        st_ref[h] = st * jnp.exp(g_last) + _dot_tn(v, kd)
        on = _rms(o, nw_ref[:, vs])
        o_ref[:, vs] = (on * sr_ref[:, vs].astype(F32)).astype(BF16)


def _gla_mixer(x, b, s, nw, w_in, w_gate_up, b_gate, norm_w, w_out):
    m = b * s
    D, H, DKT, R = D_MODEL, GLA_HEADS, GLA_DK_TOTAL, GLA_RANK
    tm = _row_tile(m)
    w = w_in.astype(BF16)
    wq, wk = w[:, :DKT], w[:, DKT:2 * DKT]
    wv, wr = w[:, 2 * DKT:2 * DKT + D], w[:, 2 * DKT + D:2 * DKT + 2 * D]
    wg = jnp.pad(w[:, 2 * DKT + 2 * D:], ((0, 0), (0, LANES - R)))
    wup = jnp.pad(w_gate_up.astype(BF16), ((0, LANES - R), (0, 0)))

    row = pl.BlockSpec((tm, D), lambda i: (i, 0))
    half = pl.BlockSpec((tm, DKT), lambda i: (i, 0))
    q, k, v, sr, la = pl.pallas_call(
        _gla_proj_kernel,
        grid=(m // tm,),
        in_specs=[row, _resident((1, D)), _resident((D, DKT)), _resident((D, DKT)), _resident((D, D)),
                  _resident((D, D)), _resident((D, LANES)), _resident((LANES, DKT)), _resident((1, DKT))],
        out_specs=[half, half, row, row, half],
        out_shape=[jax.ShapeDtypeStruct((m, DKT), BF16), jax.ShapeDtypeStruct((m, DKT), BF16),
                   jax.ShapeDtypeStruct((m, D), BF16), jax.ShapeDtypeStruct((m, D), BF16),
                   jax.ShapeDtypeStruct((m, DKT), F32)],
        compiler_params=_params(("parallel",)),
        name="gla_proj",
    )(x, nw, wq, wk, wv, wr, wg, wup, b_gate.reshape(1, DKT))

    L = min(GLA_CHUNK, s)
    levels = L.bit_length() - 1
    nc = s // L
    lvl = jnp.asarray(_gla_level_matrix(L, levels), dtype=BF16)
    kblk = pl.BlockSpec((L, DKT), lambda i, j: (i * nc + j, 0))
    vblk = pl.BlockSpec((L, D), lambda i, j: (i * nc + j, 0))
    og = pl.pallas_call(
        _gla_core_kernel,
        grid=(b, nc),
        in_specs=[kblk, kblk, vblk, vblk, kblk, _resident(((levels + 1) * L, L)), _resident((1, D))],
        out_specs=vblk,
        out_shape=jax.ShapeDtypeStruct((m, D), BF16),
        scratch_shapes=[pltpu.VMEM((H, GLA_DV, GLA_DK), F32)],
        compiler_params=_params(("parallel", "arbitrary")),
        name="gla_core",
    )(q, k, v, sr, la, lvl, norm_w.reshape(1, D))
    return _outproj(x, og, w_out.astype(BF16))


def kernel(x, norm_w, ffn_w_in, ffn_w_out, mlstm_w_in, mlstm_b_gate, mlstm_conv_w, mlstm_norm_w, mlstm_w_out,
           fox_w_in, fox_b_f, fox_qk_norm, fox_w_out, gla_w_in, gla_w_gate_up, gla_b_gate, gla_norm_w,
           gla_w_out, final_norm_w):
    b, s, d = x.shape
    depth = norm_w.shape[0]
    xf = x.reshape(b * s, d)
    fw = final_norm_w.reshape(1, d)
    for layer in range(depth):
        kind, j = layer % N_MIXERS, layer // N_MIXERS
        xf = _ffn(xf, norm_w[layer, 0].reshape(1, d), ffn_w_in[layer, 0].astype(BF16),
                  ffn_w_out[layer, 0].astype(BF16), fw, False)
        nw = norm_w[layer, 1].reshape(1, d)
        if kind == 0:
            xf = _mlstm_mixer(xf, b, s, nw, mlstm_w_in[j], mlstm_b_gate[j], mlstm_conv_w[j], mlstm_norm_w[j],
                              mlstm_w_out[j])
        elif kind == 1:
            xf = _fox_mixer(xf, b, s, nw, fox_w_in[j], fox_b_f[j], fox_qk_norm[j], fox_w_out[j])
        else:
            xf = _gla_mixer(xf, b, s, nw, gla_w_in[j], gla_w_gate_up[j], gla_b_gate[j], gla_norm_w[j],
                            gla_w_out[j])
        xf = _ffn(xf, norm_w[layer, 2].reshape(1, d), ffn_w_in[layer, 1].astype(BF16),
                  ffn_w_out[layer, 1].astype(BF16), fw, layer == depth - 1)
    return xf.reshape(b, s, d)
```

```python
import functools

import numpy as np
import jax
import jax.numpy as jnp
from jax import lax
from jax.experimental import pallas as pl
from jax.experimental.pallas import tpu as pltpu

F32 = jnp.float32
BF16 = jnp.bfloat16

EPS = 1e-6
D_MODEL = 1024
D_FF = 2816
MLSTM_HEADS = 4
MLSTM_DH = D_MODEL // MLSTM_HEADS
CONV_WIDTH = 4
FOX_HEADS = 8
FOX_DH = D_MODEL // FOX_HEADS
GLA_HEADS = 4
GLA_DK_TOTAL = D_MODEL // 2
GLA_DK = GLA_DK_TOTAL // GLA_HEADS
GLA_DV = D_MODEL // GLA_HEADS
GLA_RANK = 16
GLA_TAU = 16.0
N_MIXERS = 3

LANES = 128
SUBLANES = 8
VMEM_LIMIT_BYTES = 56 * 1024 * 1024

ROW_TILE = 512
MLSTM_CHUNK = 256
MLSTM_CONV_COLS = 512
GLA_CHUNK = 128
GLA_LEVELS = 7
ATTN_TILE = 512
ATTN_HEADS_PER_STEP = 2
LOG2E = 1.4426950408889634


def _dot(a, b):
    return jnp.dot(a, b, preferred_element_type=F32)


def _dot_nt(a, b):
    return lax.dot_general(a, b, (((1,), (1,)), ((), ())), preferred_element_type=F32)


def _dot_tn(a, b):
    return lax.dot_general(a, b, (((0,), (0,)), ((), ())), preferred_element_type=F32)


def _rms(x, w):
    ms = jnp.mean(x * x, axis=-1, keepdims=True)
    return x * lax.rsqrt(ms + EPS) * w


def _sigmoid(x):
    return 1.0 / (1.0 + jnp.exp(-x))


def _log_sigmoid(x):
    return jnp.minimum(x, 0.0) - jnp.log1p(jnp.exp(-jnp.abs(x)))


def _split3(x):
    hi = x.astype(BF16)
    r1 = x - hi.astype(F32)
    mid = r1.astype(BF16)
    lo = (r1 - mid.astype(F32)).astype(BF16)
    return hi, mid, lo


def _cumsum_rows(tril, x):
    hi, mid, lo = _split3(x)
    return _dot(tril, hi) + _dot(tril, mid) + _dot(tril, lo)


def _cumsum_lanes(x, triu):
    hi, mid, lo = _split3(x)
    return _dot(hi, triu) + _dot(mid, triu) + _dot(lo, triu)


def _tri(n):
    row = lax.broadcasted_iota(jnp.int32, (n, n), 0)
    col = lax.broadcasted_iota(jnp.int32, (n, n), 1)
    tril = jnp.where(row >= col, 1.0, 0.0).astype(BF16)
    triu = jnp.where(row <= col, 1.0, 0.0).astype(BF16)
    return row, col, tril, triu


def _resident(shape):
    zeros = (0,) * len(shape)
    return pl.BlockSpec(shape, lambda *_: zeros, pipeline_mode=pl.Buffered(1))


def _params(semantics):
    return pltpu.CompilerParams(dimension_semantics=semantics, vmem_limit_bytes=VMEM_LIMIT_BYTES)


def _row_tile(s):
    return min(ROW_TILE, s)


def _ffn_kernel(x_ref, nw_ref, win_ref, wout_ref, fw_ref, o_ref, *, final):
    x = x_ref[...]
    xn = _rms(x, nw_ref[...]).astype(BF16)
    g = _dot(xn, win_ref[:, :D_FF])
    u = _dot(xn, win_ref[:, D_FF:])
    a = (g * _sigmoid(g) * u).astype(BF16)
    y = x + 0.5 * _dot(a, wout_ref[...])
    if final:
        y = _rms(y, fw_ref[...])
    o_ref[...] = y


def _ffn(x, nw, w_in, w_out, fw, final):
    m = x.shape[0]
    tm = _row_tile(m)
    row = pl.BlockSpec((tm, D_MODEL), lambda i: (i, 0))
    return pl.pallas_call(
        functools.partial(_ffn_kernel, final=final),
        grid=(m // tm,),
        in_specs=[row, _resident((1, D_MODEL)), _resident((D_MODEL, 2 * D_FF)),
                  _resident((D_FF, D_MODEL)), _resident((1, D_MODEL))],
        out_specs=row,
        out_shape=jax.ShapeDtypeStruct((m, D_MODEL), F32),
        compiler_params=_params(("parallel",)),
        name="ffn_final" if final else "ffn",
    )(x, nw, w_in, w_out, fw)


def _outproj_kernel(x_ref, a_ref, w_ref, o_ref):
    o_ref[...] = x_ref[...] + _dot(a_ref[...], w_ref[...])


def _outproj(x, a, w):
    m = x.shape[0]
    tm = _row_tile(m)
    row = pl.BlockSpec((tm, D_MODEL), lambda i: (i, 0))
    return pl.pallas_call(
        _outproj_kernel,
        grid=(m // tm,),
        in_specs=[row, row, _resident((D_MODEL, D_MODEL))],
        out_specs=row,
        out_shape=jax.ShapeDtypeStruct((m, D_MODEL), F32),
        compiler_params=_params(("parallel",)),
        name="outproj",
    )(x, a, w)


def _mlstm_proj_kernel(x_ref, nw_ref, wqk_ref, wv_ref, wo_ref, wg_ref, wgt_ref, bcol_ref, brow_ref, cw_ref,
                       q_ref, k_ref, v_ref, so_ref, gcol_ref, grow_ref, carry_ref):
    tm = x_ref.shape[0]

    @pl.when(pl.program_id(1) == 0)
    def _():
        carry_ref[:SUBLANES, :] = jnp.zeros((SUBLANES, carry_ref.shape[1]), F32)

    xn = _rms(x_ref[...], nw_ref[...]).astype(BF16)
    for c in range(2 * D_MODEL // MLSTM_CONV_COLS):
        cs = slice(c * MLSTM_CONV_COLS, (c + 1) * MLSTM_CONV_COLS)
        carry_ref[SUBLANES:, cs] = _dot(xn, wqk_ref[:, cs])
        cw = cw_ref[:, cs]
        conv = carry_ref[SUBLANES:, cs] * cw[CONV_WIDTH - 1:CONV_WIDTH]
        for j in range(1, CONV_WIDTH):
            conv = conv + carry_ref[SUBLANES - j:SUBLANES - j + tm, cs] * cw[CONV_WIDTH - 1 - j:CONV_WIDTH - j]
        carry_ref[:SUBLANES, cs] = carry_ref[tm:, cs]
        act = conv * _sigmoid(conv)
        if c * MLSTM_CONV_COLS < D_MODEL:
            q_ref[:, cs] = (act * MLSTM_DH ** -0.5).astype(BF16)
        else:
            k_ref[:, slice(cs.start - D_MODEL, cs.stop - D_MODEL)] = act.astype(BF16)
    v_ref[...] = _dot(xn, wv_ref[...]).astype(BF16)
    so_ref[...] = _sigmoid(_dot(xn, wo_ref[...])).astype(BF16)
    gcol_ref[...] = _dot(xn, wg_ref[...]) + bcol_ref[...]
    grow_ref[...] = _dot_nt(wgt_ref[...], xn) + brow_ref[...]


def _select_lane(parts, sel):
    return _dot(parts[0], sel) + _dot(parts[1], sel) + _dot(parts[2], sel)


def _mlstm_core_kernel(q_ref, k_ref, v_ref, so_ref, gcol_ref, grow_ref, nw_ref, o_ref, cn_ref, m_ref):
    L = q_ref.shape[0]
    H, DH = MLSTM_HEADS, MLSTM_DH
    nl = L // LANES

    @pl.when(pl.program_id(1) == 0)
    def _():
        cn_ref[...] = jnp.zeros_like(cn_ref)
        m_ref[...] = jnp.zeros_like(m_ref)

    _, _, tril, triu = _tri(L)
    row = lax.broadcasted_iota(jnp.int32, (L, LANES), 0)
    lane = lax.broadcasted_iota(jnp.int32, (L, LANES), 1)
    sel_row = lax.broadcasted_iota(jnp.int32, (LANES, LANES), 0)
    ones = jnp.ones((L, LANES), BF16)
    gcol = gcol_ref[...]
    grow = grow_ref[...]
    bc_all = _cumsum_rows(tril, _log_sigmoid(gcol))
    br_all = _cumsum_lanes(_log_sigmoid(grow), triu)
    gcol_parts = _split3(gcol)
    bc_parts = _split3(bc_all)

    for h in range(H):
        hs = slice(h * DH, (h + 1) * DH)
        q = q_ref[:, hs]
        k = k_ref[:, hs]
        v_aug = jnp.concatenate([v_ref[:, hs], ones], axis=1)
        icb = _select_lane(gcol_parts, jnp.where(sel_row == h, 1.0, 0.0).astype(BF16))
        bcb = _select_lane(bc_parts, jnp.where(sel_row == H + h, 1.0, 0.0).astype(BF16))
        a_row = grow[h:h + 1, :] - br_all[H + h:H + h + 1, :]
        m_prev = m_ref[h]
        cn_prev = cn_ref[h]

        a_tiles = [jnp.where(row >= lane + j * LANES, a_row[:, j * LANES:(j + 1) * LANES], -jnp.inf)
                   for j in range(nl)]
        a_max = a_tiles[0]
        for j in range(1, nl):
            a_max = jnp.maximum(a_max, a_tiles[j])
        u = jnp.maximum(m_prev, jnp.max(a_max, axis=-1, keepdims=True))
        inter = jnp.exp(m_prev - u)
        s = _dot_nt(q, k)
        sc = jnp.concatenate([s[:, j * LANES:(j + 1) * LANES] * jnp.exp(a_tiles[j] - u) for j in range(nl)],
                             axis=1)
        tot = _dot(sc.astype(BF16), v_aug) + jnp.concatenate([inter] * (DH // LANES + 1), axis=1) * _dot(
            q, cn_prev.astype(BF16))
        den = tot[:, DH:]
        r = 1.0 / jnp.maximum(jnp.abs(den), jnp.exp(-(bcb + u)))
        h_out = tot[:, :DH] * jnp.concatenate([r] * (DH // LANES), axis=1)

        b_last = bcb[L - 1:L, :]
        m_new = b_last + jnp.maximum(m_prev, jnp.max(a_row, axis=-1, keepdims=True))
        decay = jnp.exp(b_last + m_prev - m_new)
        w = jnp.exp(b_last - bcb + icb - m_new)
        kw = (k.astype(F32) * jnp.concatenate([w] * (DH // LANES), axis=1)).astype(BF16)
        cn_ref[h] = jnp.concatenate([decay] * (DH // LANES + 1), axis=1) * cn_prev + _dot_tn(kw, v_aug)
        m_ref[h] = m_new

        hn = _rms(h_out, nw_ref[:, hs])
        o_ref[:, hs] = (hn * so_ref[:, hs].astype(F32)).astype(BF16)


def _mlstm_mixer(x, b, s, nw, w_in, b_gate, conv_w, norm_w, w_out):
    m = b * s
    D, H = D_MODEL, MLSTM_HEADS
    tm = _row_tile(s)
    nt = s // tm
    w = w_in.astype(BF16)
    wqk, wv, wo = w[:, :2 * D], w[:, 2 * D:3 * D], w[:, 3 * D:4 * D]
    wg = jnp.pad(w[:, 4 * D:], ((0, 0), (0, LANES - 2 * H)))
    wgt = w[:, 4 * D:].T
    bcol = jnp.pad(b_gate, (0, LANES - 2 * H)).reshape(1, LANES)
    brow = b_gate.reshape(2 * H, 1)

    row = pl.BlockSpec((tm, D), lambda i, j: (i * nt + j, 0))
    q, k, v, so, gcol, grow = pl.pallas_call(
        _mlstm_proj_kernel,
        grid=(b, nt),
        in_specs=[row, _resident((1, D)), _resident((D, 2 * D)), _resident((D, D)), _resident((D, D)),
                  _resident((D, LANES)), _resident((2 * H, D)), _resident((1, LANES)), _resident((2 * H, 1)),
                  _resident((CONV_WIDTH, 2 * D))],
        out_specs=[row, row, row, row,
                   pl.BlockSpec((tm, LANES), lambda i, j: (i * nt + j, 0)),
                   pl.BlockSpec((2 * H, tm), lambda i, j: (0, i * nt + j))],
        out_shape=[jax.ShapeDtypeStruct((m, D), BF16)] * 4 + [
            jax.ShapeDtypeStruct((m, LANES), F32), jax.ShapeDtypeStruct((2 * H, m), F32)],
        scratch_shapes=[pltpu.VMEM((SUBLANES + tm, 2 * D), F32)],
        compiler_params=_params(("parallel", "arbitrary")),
        name="mlstm_proj",
    )(x, nw, wqk, wv, wo, wg, wgt, bcol, brow, conv_w)

    L = min(MLSTM_CHUNK, s)
    nc = s // L
    blk = pl.BlockSpec((L, D), lambda i, j: (i * nc + j, 0))
    hg = pl.pallas_call(
        _mlstm_core_kernel,
        grid=(b, nc),
        in_specs=[blk, blk, blk, blk,
                  pl.BlockSpec((L, LANES), lambda i, j: (i * nc + j, 0)),
                  pl.BlockSpec((2 * H, L), lambda i, j: (0, i * nc + j)),
                  _resident((1, D))],
        out_specs=blk,
        out_shape=jax.ShapeDtypeStruct((m, D), BF16),
        scratch_shapes=[pltpu.VMEM((H, MLSTM_DH, MLSTM_DH + LANES), F32), pltpu.VMEM((H, 1, LANES), F32)],
        compiler_params=_params(("parallel", "arbitrary")),
        name="mlstm_core",
    )(q, k, v, so, gcol, grow, norm_w.reshape(1, D))
    return _outproj(x, hg, w_out.astype(BF16))


def _fox_proj_kernel(x_ref, nw_ref, wq_ref, wk_ref, wv_ref, wo_ref, wf_ref, wft_ref, bcol_ref, brow_ref, qkn_ref,
                     q_ref, k_ref, v_ref, so_ref, fcol_ref, frow_ref, ccol_ref, crow_ref):
    tm = x_ref.shape[0]

    @pl.when(pl.program_id(1) == 0)
    def _():
        ccol_ref[...] = jnp.zeros_like(ccol_ref)
        crow_ref[...] = jnp.zeros_like(crow_ref)

    xn = _rms(x_ref[...], nw_ref[...]).astype(BF16)
    q = _dot(xn, wq_ref[...])
    k = _dot(xn, wk_ref[...])
    qkn = qkn_ref[...]
    for h in range(FOX_HEADS):
        hs = slice(h * FOX_DH, (h + 1) * FOX_DH)
        q_ref[:, hs] = (_rms(q[:, hs], qkn[0:1]) * (FOX_DH ** -0.5 * LOG2E)).astype(BF16)
        k_ref[:, hs] = _rms(k[:, hs], qkn[1:2]).astype(BF16)
    v_ref[...] = _dot(xn, wv_ref[...]).astype(BF16)
    so_ref[...] = _sigmoid(_dot(xn, wo_ref[...])).astype(BF16)

    _, _, tril, triu = _tri(tm)
    lf_col = _log_sigmoid(_dot(xn, wf_ref[...]) + bcol_ref[...])
    lf_row = _log_sigmoid(_dot_nt(wft_ref[...], xn) + brow_ref[...])
    f_col = _cumsum_rows(tril, lf_col) + ccol_ref[...]
    f_row = _cumsum_lanes(lf_row, triu) + crow_ref[...]
    fcol_ref[...] = f_col * LOG2E
    for h in range(FOX_HEADS):
        frow_ref[h] = f_row[h:h + 1, :] * LOG2E
    ccol_ref[...] = f_col[tm - 1:tm, :]
    crow_ref[...] = f_row[:, tm - 1:tm]


def _fox_attn_kernel(q_ref, k_ref, v_ref, so_ref, fcol_ref, frow_ref, o_ref,
                     s_ref, m_ref, l_ref, acc_ref, fq_ref):
    t = q_ref.shape[0]
    G = s_ref.shape[0]
    nl = t // LANES
    hp = pl.program_id(1)
    qi = pl.program_id(2)

    lane = lax.broadcasted_iota(jnp.int32, (t, LANES), 1)
    row = lax.broadcasted_iota(jnp.int32, (t, LANES), 0)
    for g in range(G):
        fq = jnp.sum(jnp.where(lane == hp * G + g, fcol_ref[...], 0.0), axis=-1, keepdims=True)
        fq_ref[g] = jnp.broadcast_to(fq, (t, LANES))
    m_ref[...] = jnp.full_like(m_ref, -jnp.inf)
    l_ref[...] = jnp.zeros_like(l_ref)
    acc_ref[...] = jnp.zeros_like(acc_ref)

    def scores(kt, diagonal):
        off = pl.multiple_of(kt * t, t)
        for g in range(G):
            hs = slice(g * FOX_DH, (g + 1) * FOX_DH)
            s = _dot_nt(q_ref[:, hs], k_ref[pl.ds(off, t), hs])
            fk = frow_ref[g, :, pl.ds(off, t)]
            fqb = fq_ref[g]
            m = m_ref[g]
            for j in range(nl):
                ls = slice(j * LANES, (j + 1) * LANES)
                sj = s[:, ls] + (fqb - fk[:, ls])
                if diagonal:
                    sj = jnp.where(row >= lane + j * LANES, sj, -jnp.inf)
                s_ref[g, :, pl.ds(pl.multiple_of(off + j * LANES, LANES), LANES)] = sj
                m = jnp.maximum(m, sj)
            m_ref[g] = m

    def scores_body(kt, carry):
        scores(kt, False)
        return carry

    lax.fori_loop(0, qi, scores_body, 0)
    scores(qi, True)
    for g in range(G):
        m_ref[g] = jnp.broadcast_to(jnp.max(m_ref[g], axis=-1, keepdims=True), (t, LANES))

    def accumulate(kt, carry):
        off = pl.multiple_of(kt * t, t)
        for g in range(G):
            hs = slice(g * FOX_DH, (g + 1) * FOX_DH)
            mb = m_ref[g]
            l = l_ref[g]
            ps = []
            for j in range(nl):
                p = jnp.exp2(s_ref[g, :, pl.ds(pl.multiple_of(off + j * LANES, LANES), LANES)] - mb)
                l = l + p
                ps.append(p.astype(BF16))
            l_ref[g] = l
            acc_ref[g] += _dot(jnp.concatenate(ps, axis=1), v_ref[pl.ds(off, t), hs])
        return carry

    lax.fori_loop(0, qi + 1, accumulate, 0)
    for g in range(G):
        hs = slice(g * FOX_DH, (g + 1) * FOX_DH)
        l = jnp.sum(l_ref[g], axis=-1, keepdims=True)
        o_ref[:, hs] = (acc_ref[g] * (1.0 / l) * so_ref[:, hs].astype(F32)).astype(BF16)


def _fox_mixer(x, b, s, nw, w_in, b_f, qk_norm, w_out):
    m = b * s
    D, H, DH = D_MODEL, FOX_HEADS, FOX_DH
    tm = _row_tile(s)
    nt = s // tm
    w = w_in.astype(BF16)
    wq, wk, wv, wo = w[:, :D], w[:, D:2 * D], w[:, 2 * D:3 * D], w[:, 3 * D:4 * D]
    wf = jnp.pad(w[:, 4 * D:], ((0, 0), (0, LANES - H)))
    wft = w[:, 4 * D:].T
    bcol = jnp.pad(b_f, (0, LANES - H)).reshape(1, LANES)
    brow = b_f.reshape(H, 1)

    row = pl.BlockSpec((tm, D), lambda i, j: (i * nt + j, 0))
    q, k, v, so, fcol, frow = pl.pallas_call(
        _fox_proj_kernel,
        grid=(b, nt),
        in_specs=[row, _resident((1, D)), _resident((D, D)), _resident((D, D)), _resident((D, D)),
                  _resident((D, D)), _resident((D, LANES)), _resident((H, D)), _resident((1, LANES)),
                  _resident((H, 1)), _resident((2, DH))],
        out_specs=[row, row, row, row,
                   pl.BlockSpec((tm, LANES), lambda i, j: (i * nt + j, 0)),
                   pl.BlockSpec((H, 1, tm), lambda i, j: (0, 0, i * nt + j))],
        out_shape=[jax.ShapeDtypeStruct((m, D), BF16)] * 4 + [
            jax.ShapeDtypeStruct((m, LANES), F32), jax.ShapeDtypeStruct((H, 1, m), F32)],
        scratch_shapes=[pltpu.VMEM((1, LANES), F32), pltpu.VMEM((H, 1), F32)],
        compiler_params=_params(("parallel", "arbitrary")),
        name="fox_proj",
    )(x, nw, wq, wk, wv, wo, wf, wft, bcol, brow, qk_norm)

    t = min(ATTN_TILE, s)
    nq = s // t
    G = ATTN_HEADS_PER_STEP
    qblk = pl.BlockSpec((t, G * DH), lambda bi, hi, qi: (bi * nq + qi, hi))
    kblk = pl.BlockSpec((s, G * DH), lambda bi, hi, qi: (bi, hi))
    att = pl.pallas_call(
        _fox_attn_kernel,
        grid=(b, H // G, nq),
        in_specs=[qblk, kblk, kblk, qblk,
                  pl.BlockSpec((t, LANES), lambda bi, hi, qi: (bi * nq + qi, 0)),
                  pl.BlockSpec((G, 1, s), lambda bi, hi, qi: (hi, 0, bi))],
        out_specs=qblk,
        out_shape=jax.ShapeDtypeStruct((m, D), BF16),
        scratch_shapes=[pltpu.VMEM((G, t, s), F32), pltpu.VMEM((G, t, LANES), F32),
                        pltpu.VMEM((G, t, LANES), F32), pltpu.VMEM((G, t, DH), F32),
                        pltpu.VMEM((G, t, LANES), F32)],
        compiler_params=_params(("parallel", "parallel", "arbitrary")),
        name="fox_attn",
    )(q, k, v, so, fcol, frow)
    return _outproj(x, att, w_out.astype(BF16))


def _gla_level_matrix(L, levels):
    t = np.arange(L)[:, None]
    c = np.arange(L)[None, :]
    blocks = [(c <= t)]
    for j in range(levels):
        hs = 1 << j
        mid = (t & ~(2 * hs - 1)) + hs
        upper = (t & hs) != 0
        blocks.append(np.where(upper, (c > mid) & (c <= t), (c > t) & (c <= mid)))
    return np.concatenate(blocks, axis=0).astype(np.float32)


def _gla_proj_kernel(x_ref, nw_ref, wq_ref, wk_ref, wv_ref, wr_ref, wg_ref, wup_ref, bg_ref,
                     q_ref, k_ref, v_ref, sr_ref, la_ref):
    xn = _rms(x_ref[...], nw_ref[...]).astype(BF16)
    q_ref[...] = (_dot(xn, wq_ref[...]) * GLA_DK ** -0.5).astype(BF16)
    k_ref[...] = _dot(xn, wk_ref[...]).astype(BF16)
    v_ref[...] = _dot(xn, wv_ref[...]).astype(BF16)
    r = _dot(xn, wr_ref[...])
    sr_ref[...] = (r * _sigmoid(r)).astype(BF16)
    glr = _dot(xn, wg_ref[...])
    z = _dot(glr.astype(BF16), wup_ref[...]) + bg_ref[...]
    la_ref[...] = _log_sigmoid(z) * (1.0 / GLA_TAU)


def _gla_core_kernel(q_ref, k_ref, v_ref, sr_ref, la_ref, lvl_ref, nw_ref, o_ref, st_ref):
    L = q_ref.shape[0]
    H, DK, DV = GLA_HEADS, GLA_DK, GLA_DV
    levels = lvl_ref.shape[0] // L - 1

    @pl.when(pl.program_id(1) == 0)
    def _():
        st_ref[...] = jnp.zeros_like(st_ref)

    hi, mid, _ = _split3(la_ref[...])
    args = _dot(lvl_ref[...], jnp.concatenate([hi, mid], axis=0))
    row = lax.broadcasted_iota(jnp.int32, (L, L), 0)
    col = lax.broadcasted_iota(jnp.int32, (L, L), 1)
    diff = jnp.where(row > col, row ^ col, 0)

    for h in range(H):
        ks = slice(h * DK, (h + 1) * DK)
        vs = slice(h * DV, (h + 1) * DV)
        q = q_ref[:, ks].astype(F32)
        k = k_ref[:, ks].astype(F32)
        v = v_ref[:, vs]
        g = args[0:L, ks]
        a = jnp.where(row == col, jnp.sum(q * k, axis=-1, keepdims=True), 0.0)
        for j in range(levels):
            e = jnp.exp(args[(j + 1) * L:(j + 2) * L, ks])
            aj = _dot_nt((q * e).astype(BF16), (k * e).astype(BF16))
            a = jnp.where((diff >> j) == 1, aj, a)
        st = st_ref[h]
        o = _dot(a.astype(BF16), v) + _dot_nt((q * jnp.exp(g)).astype(BF16), st.astype(BF16))
        g_last = g[L - 1:L, :]
        kd = (k * jnp.exp(g_last - g)).astype(BF16)
        st_ref[h] = st * jnp.exp(g_last) + _dot_tn(v, kd)
        on = _rms(o, nw_ref[:, vs])
        o_ref[:, vs] = (on * sr_ref[:, vs].astype(F32)).astype(BF16)


def _gla_mixer(x, b, s, nw, w_in, w_gate_up, b_gate, norm_w, w_out):
    m = b * s
    D, H, DKT, R = D_MODEL, GLA_HEADS, GLA_DK_TOTAL, GLA_RANK
    tm = _row_tile(m)
    w = w_in.astype(BF16)
    wq, wk = w[:, :DKT], w[:, DKT:2 * DKT]
    wv, wr = w[:, 2 * DKT:2 * DKT + D], w[:, 2 * DKT + D:2 * DKT + 2 * D]
    wg = jnp.pad(w[:, 2 * DKT + 2 * D:], ((0, 0), (0, LANES - R)))
    wup = jnp.pad(w_gate_up.astype(BF16), ((0, LANES - R), (0, 0)))

    row = pl.BlockSpec((tm, D), lambda i: (i, 0))
    half = pl.BlockSpec((tm, DKT), lambda i: (i, 0))
    q, k, v, sr, la = pl.pallas_call(
        _gla_proj_kernel,
        grid=(m // tm,),
        in_specs=[row, _resident((1, D)), _resident((D, DKT)), _resident((D, DKT)), _resident((D, D)),
                  _resident((D, D)), _resident((D, LANES)), _resident((LANES, DKT)), _resident((1, DKT))],
        out_specs=[half, half, row, row, half],
        out_shape=[jax.ShapeDtypeStruct((m, DKT), BF16), jax.ShapeDtypeStruct((m, DKT), BF16),
                   jax.ShapeDtypeStruct((m, D), BF16), jax.ShapeDtypeStruct((m, D), BF16),
                   jax.ShapeDtypeStruct((m, DKT), F32)],
        compiler_params=_params(("parallel",)),
        name="gla_proj",
    )(x, nw, wq, wk, wv, wr, wg, wup, b_gate.reshape(1, DKT))

    L = min(GLA_CHUNK, s)
    levels = L.bit_length() - 1
    nc = s // L
    lvl = _gla_level_matrix(L, levels)
    lvl = jnp.asarray(np.concatenate([lvl, lvl], axis=1), dtype=BF16)
    kblk = pl.BlockSpec((L, DKT), lambda i, j: (i * nc + j, 0))
    vblk = pl.BlockSpec((L, D), lambda i, j: (i * nc + j, 0))
    og = pl.pallas_call(
        _gla_core_kernel,
        grid=(b, nc),
        in_specs=[kblk, kblk, vblk, vblk, kblk, _resident(((levels + 1) * L, 2 * L)), _resident((1, D))],
        out_specs=vblk,
        out_shape=jax.ShapeDtypeStruct((m, D), BF16),
        scratch_shapes=[pltpu.VMEM((H, GLA_DV, GLA_DK), F32)],
        compiler_params=_params(("parallel", "arbitrary")),
        name="gla_core",
    )(q, k, v, sr, la, lvl, norm_w.reshape(1, D))
    return _outproj(x, og, w_out.astype(BF16))


def kernel(x, norm_w, ffn_w_in, ffn_w_out, mlstm_w_in, mlstm_b_gate, mlstm_conv_w, mlstm_norm_w, mlstm_w_out,
           fox_w_in, fox_b_f, fox_qk_norm, fox_w_out, gla_w_in, gla_w_gate_up, gla_b_gate, gla_norm_w,
           gla_w_out, final_norm_w):
    b, s, d = x.shape
    depth = norm_w.shape[0]
    xf = x.reshape(b * s, d)
    fw = final_norm_w.reshape(1, d)
    for layer in range(depth):
        kind, j = layer % N_MIXERS, layer // N_MIXERS
        xf = _ffn(xf, norm_w[layer, 0].reshape(1, d), ffn_w_in[layer, 0].astype(BF16),
                  ffn_w_out[layer, 0].astype(BF16), fw, False)
        nw = norm_w[layer, 1].reshape(1, d)
        if kind == 0:
            xf = _mlstm_mixer(xf, b, s, nw, mlstm_w_in[j], mlstm_b_gate[j], mlstm_conv_w[j], mlstm_norm_w[j],
                              mlstm_w_out[j])
        elif kind == 1:
            xf = _fox_mixer(xf, b, s, nw, fox_w_in[j], fox_b_f[j], fox_qk_norm[j], fox_w_out[j])
        else:
            xf = _gla_mixer(xf, b, s, nw, gla_w_in[j], gla_w_gate_up[j], gla_b_gate[j], gla_norm_w[j],
                            gla_w_out[j])
        xf = _ffn(xf, norm_w[layer, 2].reshape(1, d), ffn_w_in[layer, 1].astype(BF16),
                  ffn_w_out[layer, 1].astype(BF16), fw, layer == depth - 1)
    return xf.reshape(b, s, d)
```

```python
import functools

import numpy as np
import jax
import jax.numpy as jnp
from jax import lax
from jax.experimental import pallas as pl
from jax.experimental.pallas import tpu as pltpu

F32 = jnp.float32
BF16 = jnp.bfloat16

EPS = 1e-6
D_MODEL = 1024
D_FF = 2816
MLSTM_HEADS = 4
MLSTM_DH = D_MODEL // MLSTM_HEADS
CONV_WIDTH = 4
FOX_HEADS = 8
FOX_DH = D_MODEL // FOX_HEADS
GLA_HEADS = 4
GLA_DK_TOTAL = D_MODEL // 2
GLA_DK = GLA_DK_TOTAL // GLA_HEADS
GLA_DV = D_MODEL // GLA_HEADS
GLA_RANK = 16
GLA_TAU = 16.0
N_MIXERS = 3

LANES = 128
SUBLANES = 8
VMEM_LIMIT_BYTES = 56 * 1024 * 1024

ROW_TILE = 512
MLSTM_CHUNK = 256
MLSTM_CONV_COLS = 512
MLSTM_ROW_TILE = 256
GLA_CHUNK = 128
GLA_LEVELS = 7
ATTN_TILE = 512
ATTN_HEADS_PER_STEP = 4
LOG2E = 1.4426950408889634


def _dot(a, b):
    return jnp.dot(a, b, preferred_element_type=F32)


def _dot_nt(a, b):
    return lax.dot_general(a, b, (((1,), (1,)), ((), ())), preferred_element_type=F32)


def _dot_tn(a, b):
    return lax.dot_general(a, b, (((0,), (0,)), ((), ())), preferred_element_type=F32)


def _rms(x, w):
    ms = jnp.mean(x * x, axis=-1, keepdims=True)
    return x * lax.rsqrt(ms + EPS) * w


def _sigmoid(x):
    return 1.0 / (1.0 + jnp.exp(-x))


def _log_sigmoid(x):
    return jnp.minimum(x, 0.0) - jnp.log1p(jnp.exp(-jnp.abs(x)))


def _split3(x):
    hi = x.astype(BF16)
    r1 = x - hi.astype(F32)
    mid = r1.astype(BF16)
    lo = (r1 - mid.astype(F32)).astype(BF16)
    return hi, mid, lo


def _cumsum_rows(tril, x):
    hi, mid, lo = _split3(x)
    return _dot(tril, hi) + _dot(tril, mid) + _dot(tril, lo)


def _cumsum_lanes(x, triu):
    hi, mid, lo = _split3(x)
    return _dot(hi, triu) + _dot(mid, triu) + _dot(lo, triu)


def _tri(n):
    row = lax.broadcasted_iota(jnp.int32, (n, n), 0)
    col = lax.broadcasted_iota(jnp.int32, (n, n), 1)
    tril = jnp.where(row >= col, 1.0, 0.0).astype(BF16)
    triu = jnp.where(row <= col, 1.0, 0.0).astype(BF16)
    return row, col, tril, triu


def _resident(shape):
    zeros = (0,) * len(shape)
    return pl.BlockSpec(shape, lambda *_: zeros, pipeline_mode=pl.Buffered(1))


def _params(semantics):
    return pltpu.CompilerParams(dimension_semantics=semantics, vmem_limit_bytes=VMEM_LIMIT_BYTES)


def _row_tile(s):
    return min(ROW_TILE, s)


def _ffn_kernel(*refs, mixed, final):
    if mixed:
        x_ref, a_ref, wmix_ref, nw_ref, win_ref, wout_ref, fw_ref, o_ref = refs
        x = x_ref[...] + _dot(a_ref[...], wmix_ref[...])
    else:
        x_ref, nw_ref, win_ref, wout_ref, fw_ref, o_ref = refs
        x = x_ref[...]
    xn = _rms(x, nw_ref[...]).astype(BF16)
    g = _dot(xn, win_ref[:, :D_FF])
    u = _dot(xn, win_ref[:, D_FF:])
    a = (g * _sigmoid(g) * u).astype(BF16)
    y = x + 0.5 * _dot(a, wout_ref[...])
    if final:
        y = _rms(y, fw_ref[...])
    o_ref[...] = y


def _ffn(x, nw, w_in, w_out, fw, final, mix=None):
    m = x.shape[0]
    tm = _row_tile(m)
    row = pl.BlockSpec((tm, D_MODEL), lambda i: (i, 0))
    weights = [_resident((1, D_MODEL)), _resident((D_MODEL, 2 * D_FF)), _resident((D_FF, D_MODEL)),
               _resident((1, D_MODEL))]
    if mix is None:
        in_specs, args = [row] + weights, (x, nw, w_in, w_out, fw)
    else:
        in_specs, args = [row, row, _resident((D_MODEL, D_MODEL))] + weights, (x, *mix, nw, w_in, w_out, fw)
    return pl.pallas_call(
        functools.partial(_ffn_kernel, mixed=mix is not None, final=final),
        grid=(m // tm,),
        in_specs=in_specs,
        out_specs=row,
        out_shape=jax.ShapeDtypeStruct((m, D_MODEL), F32),
        compiler_params=_params(("parallel",)),
        name="ffn" + ("_mix" if mix is not None else "") + ("_final" if final else ""),
    )(*args)


def _mlstm_proj_kernel(x_ref, nw_ref, wqk_ref, wv_ref, wo_ref, wg_ref, wgt_ref, bcol_ref, brow_ref, cw_ref,
                       q_ref, k_ref, v_ref, so_ref, gcol_ref, grow_ref, carry_ref):
    tm = x_ref.shape[0]

    @pl.when(pl.program_id(1) == 0)
    def _():
        carry_ref[:SUBLANES, :] = jnp.zeros((SUBLANES, carry_ref.shape[1]), F32)

    xn = _rms(x_ref[...], nw_ref[...]).astype(BF16)
    for c in range(2 * D_MODEL // MLSTM_CONV_COLS):
        cs = slice(c * MLSTM_CONV_COLS, (c + 1) * MLSTM_CONV_COLS)
        carry_ref[SUBLANES:, cs] = _dot(xn, wqk_ref[:, cs])
        cw = cw_ref[:, cs]
        conv = carry_ref[SUBLANES:, cs] * cw[CONV_WIDTH - 1:CONV_WIDTH]
        for j in range(1, CONV_WIDTH):
            conv = conv + carry_ref[SUBLANES - j:SUBLANES - j + tm, cs] * cw[CONV_WIDTH - 1 - j:CONV_WIDTH - j]
        carry_ref[:SUBLANES, cs] = carry_ref[tm:, cs]
        act = conv * _sigmoid(conv)
        if c * MLSTM_CONV_COLS < D_MODEL:
            q_ref[:, cs] = (act * MLSTM_DH ** -0.5).astype(BF16)
        else:
            k_ref[:, slice(cs.start - D_MODEL, cs.stop - D_MODEL)] = act.astype(BF16)
    v_ref[...] = _dot(xn, wv_ref[...]).astype(BF16)
    so_ref[...] = _sigmoid(_dot(xn, wo_ref[...])).astype(BF16)
    gcol_ref[...] = _dot(xn, wg_ref[...]) + bcol_ref[...]
    grow_ref[...] = _dot_nt(wgt_ref[...], xn) + brow_ref[...]


def _select_lane(parts, sel):
    return _dot(parts[0], sel) + _dot(parts[1], sel) + _dot(parts[2], sel)


def _mlstm_core_kernel(q_ref, k_ref, v_ref, so_ref, gcol_ref, grow_ref, nw_ref, o_ref, cn_ref, m_ref):
    L = q_ref.shape[0]
    H, DH = MLSTM_HEADS, MLSTM_DH
    nl = L // LANES

    @pl.when(pl.program_id(1) == 0)
    def _():
        cn_ref[...] = jnp.zeros_like(cn_ref)
        m_ref[...] = jnp.zeros_like(m_ref)

    _, _, tril, triu = _tri(L)
    row = lax.broadcasted_iota(jnp.int32, (L, LANES), 0)
    lane = lax.broadcasted_iota(jnp.int32, (L, LANES), 1)
    sel_row = lax.broadcasted_iota(jnp.int32, (LANES, LANES), 0)
    ones = jnp.ones((L, LANES), BF16)
    gcol = gcol_ref[...]
    grow = grow_ref[...]
    bc_all = _cumsum_rows(tril, _log_sigmoid(gcol))
    br_all = _cumsum_lanes(_log_sigmoid(grow), triu)
    gcol_parts = _split3(gcol)
    bc_parts = _split3(bc_all)

    for h in range(H):
        hs = slice(h * DH, (h + 1) * DH)
        q = q_ref[:, hs]
        k = k_ref[:, hs]
        v_aug = jnp.concatenate([v_ref[:, hs], ones], axis=1)
        icb = _select_lane(gcol_parts, jnp.where(sel_row == h, 1.0, 0.0).astype(BF16))
        bcb = _select_lane(bc_parts, jnp.where(sel_row == H + h, 1.0, 0.0).astype(BF16))
        a_row = grow[h:h + 1, :] - br_all[H + h:H + h + 1, :]
        m_prev = m_ref[h]
        cn_prev = cn_ref[h]

        a_tiles = [jnp.where(row >= lane + j * LANES, a_row[:, j * LANES:(j + 1) * LANES], -jnp.inf)
                   for j in range(nl)]
        a_max = a_tiles[0]
        for j in range(1, nl):
            a_max = jnp.maximum(a_max, a_tiles[j])
        u = jnp.maximum(m_prev, jnp.max(a_max, axis=-1, keepdims=True))
        inter = jnp.exp(m_prev - u)
        s = _dot_nt(q, k)
        sc = jnp.concatenate([s[:, j * LANES:(j + 1) * LANES] * jnp.exp(a_tiles[j] - u) for j in range(nl)],
                             axis=1)
        tot = _dot(sc.astype(BF16), v_aug) + jnp.concatenate([inter] * (DH // LANES + 1), axis=1) * _dot(
            q, cn_prev.astype(BF16))
        den = tot[:, DH:]
        r = 1.0 / jnp.maximum(jnp.abs(den), jnp.exp(-(bcb + u)))
        h_out = tot[:, :DH] * jnp.concatenate([r] * (DH // LANES), axis=1)

        b_last = bcb[L - 1:L, :]
        m_new = b_last + jnp.maximum(m_prev, jnp.max(a_row, axis=-1, keepdims=True))
        decay = jnp.exp(b_last + m_prev - m_new)
        w = jnp.exp(b_last - bcb + icb - m_new)
        kw = (k.astype(F32) * jnp.concatenate([w] * (DH // LANES), axis=1)).astype(BF16)
        cn_ref[h] = jnp.concatenate([decay] * (DH // LANES + 1), axis=1) * cn_prev + _dot_tn(kw, v_aug)
        m_ref[h] = m_new

        hn = _rms(h_out, nw_ref[:, hs])
        o_ref[:, hs] = (hn * so_ref[:, hs].astype(F32)).astype(BF16)


def _mlstm_mixer(x, b, s, nw, w_in, b_gate, conv_w, norm_w, w_out):
    m = b * s
    D, H = D_MODEL, MLSTM_HEADS
    tm = min(MLSTM_ROW_TILE, s)
    nt = s // tm
    w = w_in.astype(BF16)
    wqk, wv, wo = w[:, :2 * D], w[:, 2 * D:3 * D], w[:, 3 * D:4 * D]
    wg = jnp.pad(w[:, 4 * D:], ((0, 0), (0, LANES - 2 * H)))
    wgt = w[:, 4 * D:].T
    bcol = jnp.pad(b_gate, (0, LANES - 2 * H)).reshape(1, LANES)
    brow = b_gate.reshape(2 * H, 1)

    row = pl.BlockSpec((tm, D), lambda i, j: (i * nt + j, 0))
    q, k, v, so, gcol, grow = pl.pallas_call(
        _mlstm_proj_kernel,
        grid=(b, nt),
        in_specs=[row, _resident((1, D)), _resident((D, 2 * D)), _resident((D, D)), _resident((D, D)),
                  _resident((D, LANES)), _resident((2 * H, D)), _resident((1, LANES)), _resident((2 * H, 1)),
                  _resident((CONV_WIDTH, 2 * D))],
        out_specs=[row, row, row, row,
                   pl.BlockSpec((tm, LANES), lambda i, j: (i * nt + j, 0)),
                   pl.BlockSpec((2 * H, tm), lambda i, j: (0, i * nt + j))],
        out_shape=[jax.ShapeDtypeStruct((m, D), BF16)] * 4 + [
            jax.ShapeDtypeStruct((m, LANES), F32), jax.ShapeDtypeStruct((2 * H, m), F32)],
        scratch_shapes=[pltpu.VMEM((SUBLANES + tm, 2 * D), F32)],
        compiler_params=_params(("parallel", "arbitrary")),
        name="mlstm_proj",
    )(x, nw, wqk, wv, wo, wg, wgt, bcol, brow, conv_w)

    L = min(MLSTM_CHUNK, s)
    nc = s // L
    blk = pl.BlockSpec((L, D), lambda i, j: (i * nc + j, 0))
    hg = pl.pallas_call(
        _mlstm_core_kernel,
        grid=(b, nc),
        in_specs=[blk, blk, blk, blk,
                  pl.BlockSpec((L, LANES), lambda i, j: (i * nc + j, 0)),
                  pl.BlockSpec((2 * H, L), lambda i, j: (0, i * nc + j)),
                  _resident((1, D))],
        out_specs=blk,
        out_shape=jax.ShapeDtypeStruct((m, D), BF16),
        scratch_shapes=[pltpu.VMEM((H, MLSTM_DH, MLSTM_DH + LANES), F32), pltpu.VMEM((H, 1, LANES), F32)],
        compiler_params=_params(("parallel", "arbitrary")),
        name="mlstm_core",
    )(q, k, v, so, gcol, grow, norm_w.reshape(1, D))
    return hg, w_out.astype(BF16)


def _fox_proj_kernel(x_ref, nw_ref, wq_ref, wk_ref, wv_ref, wo_ref, wf_ref, wft_ref, bcol_ref, brow_ref, qkn_ref,
                     q_ref, k_ref, v_ref, so_ref, fcol_ref, frow_ref, ccol_ref, crow_ref):
    tm = x_ref.shape[0]

    @pl.when(pl.program_id(1) == 0)
    def _():
        ccol_ref[...] = jnp.zeros_like(ccol_ref)
        crow_ref[...] = jnp.zeros_like(crow_ref)

    xn = _rms(x_ref[...], nw_ref[...]).astype(BF16)
    q = _dot(xn, wq_ref[...])
    k = _dot(xn, wk_ref[...])
    qkn = qkn_ref[...]
    for h in range(FOX_HEADS):
        hs = slice(h * FOX_DH, (h + 1) * FOX_DH)
        q_ref[:, hs] = (_rms(q[:, hs], qkn[0:1]) * (FOX_DH ** -0.5 * LOG2E)).astype(BF16)
        k_ref[:, hs] = _rms(k[:, hs], qkn[1:2]).astype(BF16)
    v_ref[...] = _dot(xn, wv_ref[...]).astype(BF16)
    so_ref[...] = _sigmoid(_dot(xn, wo_ref[...])).astype(BF16)

    _, _, tril, triu = _tri(tm)
    lf_col = _log_sigmoid(_dot(xn, wf_ref[...]) + bcol_ref[...])
    lf_row = _log_sigmoid(_dot_nt(wft_ref[...], xn) + brow_ref[...])
    f_col = _cumsum_rows(tril, lf_col) + ccol_ref[...]
    f_row = _cumsum_lanes(lf_row, triu) + crow_ref[...]
    fcol_ref[...] = f_col * LOG2E
    for h in range(FOX_HEADS):
        frow_ref[h] = f_row[h:h + 1, :] * LOG2E
    ccol_ref[...] = f_col[tm - 1:tm, :]
    crow_ref[...] = f_row[:, tm - 1:tm]


def _fox_attn_kernel(q_ref, k_ref, v_ref, so_ref, fcol_ref, frow_ref, o_ref,
                     s_ref, m_ref, l_ref, acc_ref, fq_ref):
    t = q_ref.shape[0]
    G = s_ref.shape[0]
    nl = t // LANES
    hp = pl.program_id(1)
    qi = pl.program_id(2)

    lane = lax.broadcasted_iota(jnp.int32, (t, LANES), 1)
    row = lax.broadcasted_iota(jnp.int32, (t, LANES), 0)
    for g in range(G):
        fq = jnp.sum(jnp.where(lane == hp * G + g, fcol_ref[...], 0.0), axis=-1, keepdims=True)
        fq_ref[g] = jnp.broadcast_to(fq, (t, LANES))
    m_ref[...] = jnp.full_like(m_ref, -jnp.inf)
    l_ref[...] = jnp.zeros_like(l_ref)
    acc_ref[...] = jnp.zeros_like(acc_ref)

    def scores(kt, diagonal):
        off = pl.multiple_of(kt * t, t)
        for g in range(G):
            hs = slice(g * FOX_DH, (g + 1) * FOX_DH)
            s = _dot_nt(q_ref[:, hs], k_ref[pl.ds(off, t), hs])
            fk = frow_ref[g, :, pl.ds(off, t)]
            fqb = fq_ref[g]
            m = m_ref[g]
            for j in range(nl):
                ls = slice(j * LANES, (j + 1) * LANES)
                sj = s[:, ls] + (fqb - fk[:, ls])
                if diagonal:
                    sj = jnp.where(row >= lane + j * LANES, sj, -jnp.inf)
                s_ref[g, :, pl.ds(pl.multiple_of(off + j * LANES, LANES), LANES)] = sj
                m = jnp.maximum(m, sj)
            m_ref[g] = m

    def scores_body(kt, carry):
        scores(kt, False)
        return carry

    lax.fori_loop(0, qi, scores_body, 0)
    scores(qi, True)
    for g in range(G):
        m_ref[g] = jnp.broadcast_to(jnp.max(m_ref[g], axis=-1, keepdims=True), (t, LANES))

    def accumulate(kt, carry):
        off = pl.multiple_of(kt * t, t)
        for g in range(G):
            hs = slice(g * FOX_DH, (g + 1) * FOX_DH)
            mb = m_ref[g]
            l = l_ref[g]
            ps = []
            for j in range(nl):
                p = jnp.exp2(s_ref[g, :, pl.ds(pl.multiple_of(off + j * LANES, LANES), LANES)] - mb)
                l = l + p
                ps.append(p.astype(BF16))
            l_ref[g] = l
            acc_ref[g] += _dot(jnp.concatenate(ps, axis=1), v_ref[pl.ds(off, t), hs])
        return carry

    lax.fori_loop(0, qi + 1, accumulate, 0)
    for g in range(G):
        hs = slice(g * FOX_DH, (g + 1) * FOX_DH)
        l = jnp.sum(l_ref[g], axis=-1, keepdims=True)
        o_ref[:, hs] = (acc_ref[g] * (1.0 / l) * so_ref[:, hs].astype(F32)).astype(BF16)


def _fox_mixer(x, b, s, nw, w_in, b_f, qk_norm, w_out):
    m = b * s
    D, H, DH = D_MODEL, FOX_HEADS, FOX_DH
    tm = _row_tile(s)
    nt = s // tm
    w = w_in.astype(BF16)
    wq, wk, wv, wo = w[:, :D], w[:, D:2 * D], w[:, 2 * D:3 * D], w[:, 3 * D:4 * D]
    wf = jnp.pad(w[:, 4 * D:], ((0, 0), (0, LANES - H)))
    wft = w[:, 4 * D:].T
    bcol = jnp.pad(b_f, (0, LANES - H)).reshape(1, LANES)
    brow = b_f.reshape(H, 1)

    row = pl.BlockSpec((tm, D), lambda i, j: (i * nt + j, 0))
    q, k, v, so, fcol, frow = pl.pallas_call(
        _fox_proj_kernel,
        grid=(b, nt),
        in_specs=[row, _resident((1, D)), _resident((D, D)), _resident((D, D)), _resident((D, D)),
                  _resident((D, D)), _resident((D, LANES)), _resident((H, D)), _resident((1, LANES)),
                  _resident((H, 1)), _resident((2, DH))],
        out_specs=[row, row, row, row,
                   pl.BlockSpec((tm, LANES), lambda i, j: (i * nt + j, 0)),
                   pl.BlockSpec((H, 1, tm), lambda i, j: (0, 0, i * nt + j))],
        out_shape=[jax.ShapeDtypeStruct((m, D), BF16)] * 4 + [
            jax.ShapeDtypeStruct((m, LANES), F32), jax.ShapeDtypeStruct((H, 1, m), F32)],
        scratch_shapes=[pltpu.VMEM((1, LANES), F32), pltpu.VMEM((H, 1), F32)],
        compiler_params=_params(("parallel", "arbitrary")),
        name="fox_proj",
    )(x, nw, wq, wk, wv, wo, wf, wft, bcol, brow, qk_norm)

    t = min(ATTN_TILE, s)
    nq = s // t
    G = ATTN_HEADS_PER_STEP
    qblk = pl.BlockSpec((t, G * DH), lambda bi, hi, qi: (bi * nq + qi, hi))
    kblk = pl.BlockSpec((s, G * DH), lambda bi, hi, qi: (bi, hi), pipeline_mode=pl.Buffered(1))
    att = pl.pallas_call(
        _fox_attn_kernel,
        grid=(b, H // G, nq),
        in_specs=[qblk, kblk, kblk, qblk,
                  pl.BlockSpec((t, LANES), lambda bi, hi, qi: (bi * nq + qi, 0)),
                  pl.BlockSpec((G, 1, s), lambda bi, hi, qi: (hi, 0, bi))],
        out_specs=qblk,
        out_shape=jax.ShapeDtypeStruct((m, D), BF16),
        scratch_shapes=[pltpu.VMEM((G, t, s), F32), pltpu.VMEM((G, t, LANES), F32),
                        pltpu.VMEM((G, t, LANES), F32), pltpu.VMEM((G, t, DH), F32),
                        pltpu.VMEM((G, t, LANES), F32)],
        compiler_params=_params(("parallel", "parallel", "arbitrary")),
        name="fox_attn",
    )(q, k, v, so, fcol, frow)
    return att, w_out.astype(BF16)


def _gla_level_matrix(L, levels):
    t = np.arange(L)[:, None]
    c = np.arange(L)[None, :]
    blocks = [(c <= t)]
    for j in range(levels):
        hs = 1 << j
        mid = (t & ~(2 * hs - 1)) + hs
        upper = (t & hs) != 0
        blocks.append(np.where(upper, (c > mid) & (c <= t), (c > t) & (c <= mid)))
    return np.concatenate(blocks, axis=0).astype(np.float32)


def _gla_proj_kernel(x_ref, nw_ref, wq_ref, wk_ref, wv_ref, wr_ref, wg_ref, wup_ref, bg_ref,
                     q_ref, k_ref, v_ref, sr_ref, la_ref):
    xn = _rms(x_ref[...], nw_ref[...]).astype(BF16)
    q_ref[...] = (_dot(xn, wq_ref[...]) * GLA_DK ** -0.5).astype(BF16)
    k_ref[...] = _dot(xn, wk_ref[...]).astype(BF16)
    v_ref[...] = _dot(xn, wv_ref[...]).astype(BF16)
    r = _dot(xn, wr_ref[...])
    sr_ref[...] = (r * _sigmoid(r)).astype(BF16)
    glr = _dot(xn, wg_ref[...])
    z = _dot(glr.astype(BF16), wup_ref[...]) + bg_ref[...]
    la_ref[...] = _log_sigmoid(z) * (1.0 / GLA_TAU)


def _gla_core_kernel(q_ref, k_ref, v_ref, sr_ref, la_ref, lvl_ref, nw_ref, o_ref, st_ref):
    L = q_ref.shape[0]
    H, DK, DV = GLA_HEADS, GLA_DK, GLA_DV
    levels = lvl_ref.shape[0] // L - 1

    @pl.when(pl.program_id(1) == 0)
    def _():
        st_ref[...] = jnp.zeros_like(st_ref)

    hi, mid, _ = _split3(la_ref[...])
    args = _dot(lvl_ref[...], jnp.concatenate([hi, mid], axis=0))
    row = lax.broadcasted_iota(jnp.int32, (L, L), 0)
    col = lax.broadcasted_iota(jnp.int32, (L, L), 1)
    diff = jnp.where(row > col, row ^ col, 0)

    for h in range(H):
        ks = slice(h * DK, (h + 1) * DK)
        vs = slice(h * DV, (h + 1) * DV)
        q = q_ref[:, ks].astype(F32)
        k = k_ref[:, ks].astype(F32)
        v = v_ref[:, vs]
        g = args[0:L, ks]
        a = jnp.where(row == col, jnp.sum(q * k, axis=-1, keepdims=True), 0.0)
        for j in range(levels):
            e = jnp.exp(args[(j + 1) * L:(j + 2) * L, ks])
            aj = _dot_nt((q * e).astype(BF16), (k * e).astype(BF16))
            a = jnp.where((diff >> j) == 1, aj, a)
        st = st_ref[h]
        o = _dot(a.astype(BF16), v) + _dot_nt((q * jnp.exp(g)).astype(BF16), st.astype(BF16))
        g_last = g[L - 1:L, :]
        kd = (k * jnp.exp(g_last - g)).astype(BF16)
        st_ref[h] = st * jnp.exp(g_last) + _dot_tn(v, kd)
        on = _rms(o, nw_ref[:, vs])
        o_ref[:, vs] = (on * sr_ref[:, vs].astype(F32)).astype(BF16)


def _gla_mixer(x, b, s, nw, w_in, w_gate_up, b_gate, norm_w, w_out):
    m = b * s
    D, H, DKT, R = D_MODEL, GLA_HEADS, GLA_DK_TOTAL, GLA_RANK
    tm = _row_tile(m)
    w = w_in.astype(BF16)
    wq, wk = w[:, :DKT], w[:, DKT:2 * DKT]
    wv, wr = w[:, 2 * DKT:2 * DKT + D], w[:, 2 * DKT + D:2 * DKT + 2 * D]
    wg = jnp.pad(w[:, 2 * DKT + 2 * D:], ((0, 0), (0, LANES - R)))
    wup = jnp.pad(w_gate_up.astype(BF16), ((0, LANES - R), (0, 0)))

    row = pl.BlockSpec((tm, D), lambda i: (i, 0))
    half = pl.BlockSpec((tm, DKT), lambda i: (i, 0))
    q, k, v, sr, la = pl.pallas_call(
        _gla_proj_kernel,
        grid=(m // tm,),
        in_specs=[row, _resident((1, D)), _resident((D, DKT)), _resident((D, DKT)), _resident((D, D)),
                  _resident((D, D)), _resident((D, LANES)), _resident((LANES, DKT)), _resident((1, DKT))],
        out_specs=[half, half, row, row, half],
        out_shape=[jax.ShapeDtypeStruct((m, DKT), BF16), jax.ShapeDtypeStruct((m, DKT), BF16),
                   jax.ShapeDtypeStruct((m, D), BF16), jax.ShapeDtypeStruct((m, D), BF16),
                   jax.ShapeDtypeStruct((m, DKT), F32)],
        compiler_params=_params(("parallel",)),
        name="gla_proj",
    )(x, nw, wq, wk, wv, wr, wg, wup, b_gate.reshape(1, DKT))

    L = min(GLA_CHUNK, s)
    levels = L.bit_length() - 1
    nc = s // L
    lvl = _gla_level_matrix(L, levels)
    lvl = jnp.asarray(np.concatenate([lvl, lvl], axis=1), dtype=BF16)
    kblk = pl.BlockSpec((L, DKT), lambda i, j: (i * nc + j, 0))
    vblk = pl.BlockSpec((L, D), lambda i, j: (i * nc + j, 0))
    og = pl.pallas_call(
        _gla_core_kernel,
        grid=(b, nc),
        in_specs=[kblk, kblk, vblk, vblk, kblk, _resident(((levels + 1) * L, 2 * L)), _resident((1, D))],
        out_specs=vblk,
        out_shape=jax.ShapeDtypeStruct((m, D), BF16),
        scratch_shapes=[pltpu.VMEM((H, GLA_DV, GLA_DK), F32)],
        compiler_params=_params(("parallel", "arbitrary")),
        name="gla_core",
    )(q, k, v, sr, la, lvl, norm_w.reshape(1, D))
    return og, w_out.astype(BF16)


def kernel(x, norm_w, ffn_w_in, ffn_w_out, mlstm_w_in, mlstm_b_gate, mlstm_conv_w, mlstm_norm_w, mlstm_w_out,
           fox_w_in, fox_b_f, fox_qk_norm, fox_w_out, gla_w_in, gla_w_gate_up, gla_b_gate, gla_norm_w,
           gla_w_out, final_norm_w):
    b, s, d = x.shape
    depth = norm_w.shape[0]
    xf = x.reshape(b * s, d)
    fw = final_norm_w.reshape(1, d)
    for layer in range(depth):
        kind, j = layer % N_MIXERS, layer // N_MIXERS
        xf = _ffn(xf, norm_w[layer, 0].reshape(1, d), ffn_w_in[layer, 0].astype(BF16),
                  ffn_w_out[layer, 0].astype(BF16), fw, False)
        nw = norm_w[layer, 1].reshape(1, d)
        if kind == 0:
            mix = _mlstm_mixer(xf, b, s, nw, mlstm_w_in[j], mlstm_b_gate[j], mlstm_conv_w[j], mlstm_norm_w[j],
                               mlstm_w_out[j])
        elif kind == 1:
            mix = _fox_mixer(xf, b, s, nw, fox_w_in[j], fox_b_f[j], fox_qk_norm[j], fox_w_out[j])
        else:
            mix = _gla_mixer(xf, b, s, nw, gla_w_in[j], gla_w_gate_up[j], gla_b_gate[j], gla_norm_w[j],
                             gla_w_out[j])
        xf = _ffn(xf, norm_w[layer, 2].reshape(1, d), ffn_w_in[layer, 1].astype(BF16),
                  ffn_w_out[layer, 1].astype(BF16), fw, layer == depth - 1, mix)
    return xf.reshape(b, s, d)
```

```python
import functools

import numpy as np
import jax
import jax.numpy as jnp
from jax import lax
from jax.experimental import pallas as pl
from jax.experimental.pallas import tpu as pltpu

F32 = jnp.float32
BF16 = jnp.bfloat16

EPS = 1e-6
D_MODEL = 1024
D_FF = 2816
MLSTM_HEADS = 4
MLSTM_DH = D_MODEL // MLSTM_HEADS
CONV_WIDTH = 4
FOX_HEADS = 8
FOX_DH = D_MODEL // FOX_HEADS
GLA_HEADS = 4
GLA_DK_TOTAL = D_MODEL // 2
GLA_DK = GLA_DK_TOTAL // GLA_HEADS
GLA_DV = D_MODEL // GLA_HEADS
GLA_RANK = 16
GLA_TAU = 16.0
N_MIXERS = 3

LANES = 128
SUBLANES = 8
VMEM_LIMIT_BYTES = 56 * 1024 * 1024

ROW_TILE = 512
FFN_ROW_TILE = 1024
MLSTM_CHUNK = 256
MLSTM_CONV_COLS = 512
MLSTM_ROW_TILE = 256
GLA_CHUNK = 128
ATTN_TILE = 512
ATTN_HEADS_PER_STEP = 4
LOG2E = 1.4426950408889634


def _dot(a, b):
    return jnp.dot(a, b, preferred_element_type=F32)


def _dot_nt(a, b):
    return lax.dot_general(a, b, (((1,), (1,)), ((), ())), preferred_element_type=F32)


def _dot_tn(a, b):
    return lax.dot_general(a, b, (((0,), (0,)), ((), ())), preferred_element_type=F32)


def _rms(x, w):
    ms = jnp.mean(x * x, axis=-1, keepdims=True)
    return x * lax.rsqrt(ms + EPS) * w


def _sigmoid(x):
    return 1.0 / (1.0 + jnp.exp(-x))


def _log_sigmoid(x):
    return jnp.minimum(x, 0.0) - jnp.log1p(jnp.exp(-jnp.abs(x)))


def _split3(x):
    hi = x.astype(BF16)
    r1 = x - hi.astype(F32)
    mid = r1.astype(BF16)
    lo = (r1 - mid.astype(F32)).astype(BF16)
    return hi, mid, lo


def _cumsum_rows(tril, x):
    hi, mid, lo = _split3(x)
    return _dot(tril, hi) + _dot(tril, mid) + _dot(tril, lo)


def _cumsum_lanes(x, triu):
    hi, mid, lo = _split3(x)
    return _dot(hi, triu) + _dot(mid, triu) + _dot(lo, triu)


def _tri(n):
    row = lax.broadcasted_iota(jnp.int32, (n, n), 0)
    col = lax.broadcasted_iota(jnp.int32, (n, n), 1)
    tril = jnp.where(row >= col, 1.0, 0.0).astype(BF16)
    triu = jnp.where(row <= col, 1.0, 0.0).astype(BF16)
    return row, col, tril, triu


def _resident(shape):
    zeros = (0,) * len(shape)
    return pl.BlockSpec(shape, lambda *_: zeros, pipeline_mode=pl.Buffered(1))


def _pick(shape, *lead):
    zeros = (0,) * len(shape)
    return pl.BlockSpec((None,) * len(lead) + tuple(shape), lambda *_: tuple(lead) + zeros,
                        pipeline_mode=pl.Buffered(1))


def _params(semantics):
    return pltpu.CompilerParams(dimension_semantics=semantics, vmem_limit_bytes=VMEM_LIMIT_BYTES)


def _row_tile(s):
    return min(ROW_TILE, s)


def _ffn_kernel(*refs, mixed, final):
    if mixed:
        x_ref, a_ref, wmix_ref, nw_ref, win_ref, wout_ref, fw_ref, o_ref = refs
        x = x_ref[...] + _dot(a_ref[...], wmix_ref[...])
    else:
        x_ref, nw_ref, win_ref, wout_ref, fw_ref, o_ref = refs
        x = x_ref[...]
    xn = _rms(x, nw_ref[...]).astype(BF16)
    g = _dot(xn, win_ref[:, :D_FF])
    u = _dot(xn, win_ref[:, D_FF:])
    a = (g * _sigmoid(g) * u).astype(BF16)
    y = x + 0.5 * _dot(a, wout_ref[...])
    if final:
        y = _rms(y, fw_ref[...])
    o_ref[...] = y


def _ffn(x, nw, w_in, w_out, sel, fw, final, mix=None):
    m = x.shape[0]
    tm = min(FFN_ROW_TILE, m)
    row = pl.BlockSpec((tm, D_MODEL), lambda i: (i, 0))
    weights = [_resident((1, D_MODEL)), _pick((D_MODEL, 2 * D_FF), *sel), _pick((D_FF, D_MODEL), *sel),
               _resident((1, D_MODEL))]
    if mix is None:
        in_specs, args = [row] + weights, (x, nw, w_in, w_out, fw)
    else:
        a, w_mix, j = mix
        in_specs = [row, row, _pick((D_MODEL, D_MODEL), j)] + weights
        args = (x, a, w_mix, nw, w_in, w_out, fw)
    return pl.pallas_call(
        functools.partial(_ffn_kernel, mixed=mix is not None, final=final),
        grid=(m // tm,),
        in_specs=in_specs,
        out_specs=row,
        out_shape=jax.ShapeDtypeStruct((m, D_MODEL), F32),
        compiler_params=_params(("parallel",)),
        name="ffn" + ("_mix" if mix is not None else "") + ("_final" if final else ""),
    )(*args)


def _mlstm_proj_kernel(x_ref, nw_ref, w_ref, wg_ref, wgt_ref, bcol_ref, brow_ref, cw_ref,
                       q_ref, k_ref, v_ref, so_ref, gcol_ref, grow_ref, carry_ref):
    tm = x_ref.shape[0]

    @pl.when(pl.program_id(1) == 0)
    def _():
        carry_ref[:SUBLANES, :] = jnp.zeros((SUBLANES, carry_ref.shape[1]), F32)

    xn = _rms(x_ref[...], nw_ref[...]).astype(BF16)
    for c in range(2 * D_MODEL // MLSTM_CONV_COLS):
        cs = slice(c * MLSTM_CONV_COLS, (c + 1) * MLSTM_CONV_COLS)
        carry_ref[SUBLANES:, cs] = _dot(xn, w_ref[:, cs])
        cw = cw_ref[:, cs]
        conv = carry_ref[SUBLANES:, cs] * cw[CONV_WIDTH - 1:CONV_WIDTH]
        for j in range(1, CONV_WIDTH):
            conv = conv + carry_ref[SUBLANES - j:SUBLANES - j + tm, cs] * cw[CONV_WIDTH - 1 - j:CONV_WIDTH - j]
        carry_ref[:SUBLANES, cs] = carry_ref[tm:, cs]
        act = conv * _sigmoid(conv)
        if c * MLSTM_CONV_COLS < D_MODEL:
            q_ref[:, cs] = (act * MLSTM_DH ** -0.5).astype(BF16)
        else:
            k_ref[:, slice(cs.start - D_MODEL, cs.stop - D_MODEL)] = act.astype(BF16)
    v_ref[...] = _dot(xn, w_ref[:, 2 * D_MODEL:3 * D_MODEL]).astype(BF16)
    so_ref[...] = _sigmoid(_dot(xn, w_ref[:, 3 * D_MODEL:4 * D_MODEL])).astype(BF16)
    gcol_ref[...] = _dot(xn, wg_ref[...]) + bcol_ref[...]
    grow_ref[...] = _dot_nt(wgt_ref[...], xn) + brow_ref[...]


def _select_lane(parts, sel):
    return _dot(parts[0], sel) + _dot(parts[1], sel) + _dot(parts[2], sel)


def _mlstm_core_kernel(q_ref, k_ref, v_ref, so_ref, gcol_ref, grow_ref, nw_ref, o_ref, cn_ref, m_ref):
    L = q_ref.shape[0]
    H, DH = MLSTM_HEADS, MLSTM_DH
    nl = L // LANES

    @pl.when(pl.program_id(1) == 0)
    def _():
        cn_ref[...] = jnp.zeros_like(cn_ref)
        m_ref[...] = jnp.zeros_like(m_ref)

    _, _, tril, triu = _tri(L)
    row = lax.broadcasted_iota(jnp.int32, (L, LANES), 0)
    lane = lax.broadcasted_iota(jnp.int32, (L, LANES), 1)
    sel_row = lax.broadcasted_iota(jnp.int32, (LANES, LANES), 0)
    ones = jnp.ones((L, LANES), BF16)
    gcol = gcol_ref[...]
    grow = grow_ref[...]
    bc_all = _cumsum_rows(tril, _log_sigmoid(gcol))
    br_all = _cumsum_lanes(_log_sigmoid(grow), triu)
    gcol_parts = _split3(gcol)
    bc_parts = _split3(bc_all)

    for h in range(H):
        hs = slice(h * DH, (h + 1) * DH)
        q = q_ref[:, hs]
        k = k_ref[:, hs]
        v_aug = jnp.concatenate([v_ref[:, hs], ones], axis=1)
        icb = _select_lane(gcol_parts, jnp.where(sel_row == h, 1.0, 0.0).astype(BF16))
        bcb = _select_lane(bc_parts, jnp.where(sel_row == H + h, 1.0, 0.0).astype(BF16))
        a_row = grow[h:h + 1, :] - br_all[H + h:H + h + 1, :]
        m_prev = m_ref[h]
        cn_prev = cn_ref[h]

        a_tiles = [jnp.where(row >= lane + j * LANES, a_row[:, j * LANES:(j + 1) * LANES], -jnp.inf)
                   for j in range(nl)]
        a_max = a_tiles[0]
        for j in range(1, nl):
            a_max = jnp.maximum(a_max, a_tiles[j])
        u = jnp.maximum(m_prev, jnp.max(a_max, axis=-1, keepdims=True))
        inter = jnp.exp(m_prev - u)
        s = _dot_nt(q, k)
        sc = jnp.concatenate([s[:, j * LANES:(j + 1) * LANES] * jnp.exp(a_tiles[j] - u) for j in range(nl)],
                             axis=1)
        tot = _dot(sc.astype(BF16), v_aug) + jnp.concatenate([inter] * (DH // LANES + 1), axis=1) * _dot(
            q, cn_prev.astype(BF16))
        den = tot[:, DH:]
        r = 1.0 / jnp.maximum(jnp.abs(den), jnp.exp(-(bcb + u)))
        h_out = tot[:, :DH] * jnp.concatenate([r] * (DH // LANES), axis=1)

        b_last = bcb[L - 1:L, :]
        m_new = b_last + jnp.maximum(m_prev, jnp.max(a_row, axis=-1, keepdims=True))
        decay = jnp.exp(b_last + m_prev - m_new)
        w = jnp.exp(b_last - bcb + icb - m_new)
        kw = (k.astype(F32) * jnp.concatenate([w] * (DH // LANES), axis=1)).astype(BF16)
        cn_ref[h] = jnp.concatenate([decay] * (DH // LANES + 1), axis=1) * cn_prev + _dot_tn(kw, v_aug)
        m_ref[h] = m_new

        hn = _rms(h_out, nw_ref[:, hs])
        o_ref[:, hs] = (hn * so_ref[:, hs].astype(F32)).astype(BF16)


def _mlstm_mixer(x, b, s, nw, w_all, jm, w_in, b_gate, conv_w, norm_w):
    m = b * s
    D, H = D_MODEL, MLSTM_HEADS
    tm = min(MLSTM_ROW_TILE, s)
    nt = s // tm
    wgate = w_in[:, 4 * D:].astype(BF16)
    wg = jnp.pad(wgate, ((0, 0), (0, LANES - 2 * H)))
    wgt = wgate.T
    bcol = jnp.pad(b_gate, (0, LANES - 2 * H)).reshape(1, LANES)
    brow = b_gate.reshape(2 * H, 1)

    row = pl.BlockSpec((tm, D), lambda i, j: (i * nt + j, 0))
    q, k, v, so, gcol, grow = pl.pallas_call(
        _mlstm_proj_kernel,
        grid=(b, nt),
        in_specs=[row, _resident((1, D)), _pick((D, 4 * D + 2 * H), jm),
                  _resident((D, LANES)), _resident((2 * H, D)), _resident((1, LANES)), _resident((2 * H, 1)),
                  _resident((CONV_WIDTH, 2 * D))],
        out_specs=[row, row, row, row,
                   pl.BlockSpec((tm, LANES), lambda i, j: (i * nt + j, 0)),
                   pl.BlockSpec((2 * H, tm), lambda i, j: (0, i * nt + j))],
        out_shape=[jax.ShapeDtypeStruct((m, D), BF16)] * 4 + [
            jax.ShapeDtypeStruct((m, LANES), F32), jax.ShapeDtypeStruct((2 * H, m), F32)],
        scratch_shapes=[pltpu.VMEM((SUBLANES + tm, 2 * D), F32)],
        compiler_params=_params(("parallel", "arbitrary")),
        name="mlstm_proj",
    )(x, nw, w_all, wg, wgt, bcol, brow, conv_w)

    L = min(MLSTM_CHUNK, s)
    nc = s // L
    blk = pl.BlockSpec((L, D), lambda i, j: (i * nc + j, 0))
    hg = pl.pallas_call(
        _mlstm_core_kernel,
        grid=(b, nc),
        in_specs=[blk, blk, blk, blk,
                  pl.BlockSpec((L, LANES), lambda i, j: (i * nc + j, 0)),
                  pl.BlockSpec((2 * H, L), lambda i, j: (0, i * nc + j)),
                  _resident((1, D))],
        out_specs=blk,
        out_shape=jax.ShapeDtypeStruct((m, D), BF16),
        scratch_shapes=[pltpu.VMEM((H, MLSTM_DH, MLSTM_DH + LANES), F32), pltpu.VMEM((H, 1, LANES), F32)],
        compiler_params=_params(("parallel", "arbitrary")),
        name="mlstm_core",
    )(q, k, v, so, gcol, grow, norm_w.reshape(1, D))
    return hg


def _fox_proj_kernel(x_ref, nw_ref, w_ref, wf_ref, wft_ref, bcol_ref, brow_ref, qkn_ref,
                     q_ref, k_ref, v_ref, so_ref, fcol_ref, frow_ref, ccol_ref, crow_ref):
    tm = x_ref.shape[0]

    @pl.when(pl.program_id(1) == 0)
    def _():
        ccol_ref[...] = jnp.zeros_like(ccol_ref)
        crow_ref[...] = jnp.zeros_like(crow_ref)

    xn = _rms(x_ref[...], nw_ref[...]).astype(BF16)
    q = _dot(xn, w_ref[:, :D_MODEL])
    k = _dot(xn, w_ref[:, D_MODEL:2 * D_MODEL])
    qkn = qkn_ref[...]
    for h in range(FOX_HEADS):
        hs = slice(h * FOX_DH, (h + 1) * FOX_DH)
        q_ref[:, hs] = (_rms(q[:, hs], qkn[0:1]) * (FOX_DH ** -0.5 * LOG2E)).astype(BF16)
        k_ref[:, hs] = _rms(k[:, hs], qkn[1:2]).astype(BF16)
    v_ref[...] = _dot(xn, w_ref[:, 2 * D_MODEL:3 * D_MODEL]).astype(BF16)
    so_ref[...] = _sigmoid(_dot(xn, w_ref[:, 3 * D_MODEL:4 * D_MODEL])).astype(BF16)

    _, _, tril, triu = _tri(tm)
    lf_col = _log_sigmoid(_dot(xn, wf_ref[...]) + bcol_ref[...])
    lf_row = _log_sigmoid(_dot_nt(wft_ref[...], xn) + brow_ref[...])
    f_col = _cumsum_rows(tril, lf_col) + ccol_ref[...]
    f_row = _cumsum_lanes(lf_row, triu) + crow_ref[...]
    fcol_ref[...] = f_col * LOG2E
    for h in range(FOX_HEADS):
        frow_ref[h] = f_row[h:h + 1, :] * LOG2E
    ccol_ref[...] = f_col[tm - 1:tm, :]
    crow_ref[...] = f_row[:, tm - 1:tm]


def _fox_attn_kernel(q_ref, k_ref, v_ref, so_ref, fcol_ref, frow_ref, o_ref,
                     s_ref, m_ref, l_ref, acc_ref, fq_ref):
    t = q_ref.shape[0]
    G = s_ref.shape[0]
    nl = t // LANES
    hp = pl.program_id(1)
    qi = pl.program_id(2)

    lane = lax.broadcasted_iota(jnp.int32, (t, LANES), 1)
    row = lax.broadcasted_iota(jnp.int32, (t, LANES), 0)
    for g in range(G):
        fq = jnp.sum(jnp.where(lane == hp * G + g, fcol_ref[...], 0.0), axis=-1, keepdims=True)
        fq_ref[g] = jnp.broadcast_to(fq, (t, LANES))
    m_ref[...] = jnp.full_like(m_ref, -jnp.inf)
    l_ref[...] = jnp.zeros_like(l_ref)
    acc_ref[...] = jnp.zeros_like(acc_ref)

    def scores(kt, diagonal):
        off = pl.multiple_of(kt * t, t)
        for g in range(G):
            hs = slice(g * FOX_DH, (g + 1) * FOX_DH)
            s = _dot_nt(q_ref[:, hs], k_ref[pl.ds(off, t), hs])
            fk = frow_ref[g, :, pl.ds(off, t)]
            fqb = fq_ref[g]
            m = m_ref[g]
            for j in range(nl):
                ls = slice(j * LANES, (j + 1) * LANES)
                sj = s[:, ls] + (fqb - fk[:, ls])
                if diagonal:
                    sj = jnp.where(row >= lane + j * LANES, sj, -jnp.inf)
                s_ref[g, :, pl.ds(pl.multiple_of(off + j * LANES, LANES), LANES)] = sj
                m = jnp.maximum(m, sj)
            m_ref[g] = m

    def scores_body(kt, carry):
        scores(kt, False)
        return carry

    lax.fori_loop(0, qi, scores_body, 0)
    scores(qi, True)
    for g in range(G):
        m_ref[g] = jnp.broadcast_to(jnp.max(m_ref[g], axis=-1, keepdims=True), (t, LANES))

    def accumulate(kt, carry):
        off = pl.multiple_of(kt * t, t)
        for g in range(G):
            hs = slice(g * FOX_DH, (g + 1) * FOX_DH)
            mb = m_ref[g]
            l = l_ref[g]
            ps = []
            for j in range(nl):
                p = jnp.exp2(s_ref[g, :, pl.ds(pl.multiple_of(off + j * LANES, LANES), LANES)] - mb)
                l = l + p
                ps.append(p.astype(BF16))
            l_ref[g] = l
            acc_ref[g] += _dot(jnp.concatenate(ps, axis=1), v_ref[pl.ds(off, t), hs])
        return carry

    lax.fori_loop(0, qi + 1, accumulate, 0)
    for g in range(G):
        hs = slice(g * FOX_DH, (g + 1) * FOX_DH)
        l = jnp.sum(l_ref[g], axis=-1, keepdims=True)
        o_ref[:, hs] = (acc_ref[g] * (1.0 / l) * so_ref[:, hs].astype(F32)).astype(BF16)


def _fox_mixer(x, b, s, nw, w_all, jm, w_in, b_f, qk_norm):
    m = b * s
    D, H, DH = D_MODEL, FOX_HEADS, FOX_DH
    tm = _row_tile(s)
    nt = s // tm
    wgate = w_in[:, 4 * D:].astype(BF16)
    wf = jnp.pad(wgate, ((0, 0), (0, LANES - H)))
    wft = wgate.T
    bcol = jnp.pad(b_f, (0, LANES - H)).reshape(1, LANES)
    brow = b_f.reshape(H, 1)

    row = pl.BlockSpec((tm, D), lambda i, j: (i * nt + j, 0))
    q, k, v, so, fcol, frow = pl.pallas_call(
        _fox_proj_kernel,
        grid=(b, nt),
        in_specs=[row, _resident((1, D)), _pick((D, 4 * D + H), jm), _resident((D, LANES)), _resident((H, D)),
                  _resident((1, LANES)), _resident((H, 1)), _resident((2, DH))],
        out_specs=[row, row, row, row,
                   pl.BlockSpec((tm, LANES), lambda i, j: (i * nt + j, 0)),
                   pl.BlockSpec((H, 1, tm), lambda i, j: (0, 0, i * nt + j))],
        out_shape=[jax.ShapeDtypeStruct((m, D), BF16)] * 4 + [
            jax.ShapeDtypeStruct((m, LANES), F32), jax.ShapeDtypeStruct((H, 1, m), F32)],
        scratch_shapes=[pltpu.VMEM((1, LANES), F32), pltpu.VMEM((H, 1), F32)],
        compiler_params=_params(("parallel", "arbitrary")),
        name="fox_proj",
    )(x, nw, w_all, wf, wft, bcol, brow, qk_norm)

    t = min(ATTN_TILE, s)
    nq = s // t
    G = ATTN_HEADS_PER_STEP
    qblk = pl.BlockSpec((t, G * DH), lambda bi, hi, qi: (bi * nq + qi, hi))
    kblk = pl.BlockSpec((s, G * DH), lambda bi, hi, qi: (bi, hi), pipeline_mode=pl.Buffered(1))
    att = pl.pallas_call(
        _fox_attn_kernel,
        grid=(b, H // G, nq),
        in_specs=[qblk, kblk, kblk, qblk,
                  pl.BlockSpec((t, LANES), lambda bi, hi, qi: (bi * nq + qi, 0)),
                  pl.BlockSpec((G, 1, s), lambda bi, hi, qi: (hi, 0, bi))],
        out_specs=qblk,
        out_shape=jax.ShapeDtypeStruct((m, D), BF16),
        scratch_shapes=[pltpu.VMEM((G, t, s), F32), pltpu.VMEM((G, t, LANES), F32),
                        pltpu.VMEM((G, t, LANES), F32), pltpu.VMEM((G, t, DH), F32),
                        pltpu.VMEM((G, t, LANES), F32)],
        compiler_params=_params(("parallel", "parallel", "arbitrary")),
        name="fox_attn",
    )(q, k, v, so, fcol, frow)
    return att


def _gla_level_matrix(L, levels):
    t = np.arange(L)[:, None]
    c = np.arange(L)[None, :]
    blocks = [(c <= t)]
    for j in range(levels):
        hs = 1 << j
        mid = (t & ~(2 * hs - 1)) + hs
        upper = (t & hs) != 0
        blocks.append(np.where(upper, (c > mid) & (c <= t), (c > t) & (c <= mid)))
    return np.concatenate(blocks, axis=0).astype(np.float32)


def _gla_proj_kernel(x_ref, nw_ref, w_ref, wg_ref, wup_ref, bg_ref,
                     q_ref, k_ref, v_ref, sr_ref, la_ref):
    xn = _rms(x_ref[...], nw_ref[...]).astype(BF16)
    DKT = GLA_DK_TOTAL
    q_ref[...] = (_dot(xn, w_ref[:, :DKT]) * GLA_DK ** -0.5).astype(BF16)
    k_ref[...] = _dot(xn, w_ref[:, DKT:2 * DKT]).astype(BF16)
    v_ref[...] = _dot(xn, w_ref[:, 2 * DKT:2 * DKT + D_MODEL]).astype(BF16)
    r = _dot(xn, w_ref[:, 2 * DKT + D_MODEL:2 * DKT + 2 * D_MODEL])
    sr_ref[...] = (r * _sigmoid(r)).astype(BF16)
    glr = _dot(xn, wg_ref[...])
    z = _dot(glr.astype(BF16), wup_ref[...]) + bg_ref[...]
    la_ref[...] = _log_sigmoid(z) * (1.0 / GLA_TAU)


def _gla_core_kernel(q_ref, k_ref, v_ref, sr_ref, la_ref, lvl_ref, nw_ref, o_ref, st_ref):
    L = q_ref.shape[0]
    H, DK, DV = GLA_HEADS, GLA_DK, GLA_DV
    levels = lvl_ref.shape[0] // L - 1

    @pl.when(pl.program_id(1) == 0)
    def _():
        st_ref[...] = jnp.zeros_like(st_ref)

    hi, mid, _ = _split3(la_ref[...])
    args = _dot(lvl_ref[...], jnp.concatenate([hi, mid], axis=0))
    row = lax.broadcasted_iota(jnp.int32, (L, L), 0)
    col = lax.broadcasted_iota(jnp.int32, (L, L), 1)
    diff = jnp.where(row > col, row ^ col, 0)

    for h in range(H):
        ks = slice(h * DK, (h + 1) * DK)
        vs = slice(h * DV, (h + 1) * DV)
        q = q_ref[:, ks].astype(F32)
        k = k_ref[:, ks].astype(F32)
        v = v_ref[:, vs]
        g = args[0:L, ks]
        a = jnp.where(row == col, jnp.sum(q * k, axis=-1, keepdims=True), 0.0)
        for j in range(levels):
            e = jnp.exp(args[(j + 1) * L:(j + 2) * L, ks])
            aj = _dot_nt((q * e).astype(BF16), (k * e).astype(BF16))
            a = jnp.where((diff >> j) == 1, aj, a)
        st = st_ref[h]
        o = _dot(a.astype(BF16), v) + _dot_nt((q * jnp.exp(g)).astype(BF16), st.astype(BF16))
        g_last = g[L - 1:L, :]
        kd = (k * jnp.exp(g_last - g)).astype(BF16)
        st_ref[h] = st * jnp.exp(g_last) + _dot_tn(v, kd)
        on = _rms(o, nw_ref[:, vs])
        o_ref[:, vs] = (on * sr_ref[:, vs].astype(F32)).astype(BF16)


def _gla_mixer(x, b, s, nw, w_all, jm, w_in, w_gate_up, b_gate, norm_w):
    m = b * s
    D, H, DKT, R = D_MODEL, GLA_HEADS, GLA_DK_TOTAL, GLA_RANK
    tm = _row_tile(m)
    wg = jnp.pad(w_in[:, 2 * DKT + 2 * D:].astype(BF16), ((0, 0), (0, LANES - R)))
    wup = jnp.pad(w_gate_up.astype(BF16), ((0, LANES - R), (0, 0)))

    row = pl.BlockSpec((tm, D), lambda i: (i, 0))
    half = pl.BlockSpec((tm, DKT), lambda i: (i, 0))
    q, k, v, sr, la = pl.pallas_call(
        _gla_proj_kernel,
        grid=(m // tm,),
        in_specs=[row, _resident((1, D)), _pick((D, 2 * DKT + 2 * D + R), jm), _resident((D, LANES)),
                  _resident((LANES, DKT)), _resident((1, DKT))],
        out_specs=[half, half, row, row, half],
        out_shape=[jax.ShapeDtypeStruct((m, DKT), BF16), jax.ShapeDtypeStruct((m, DKT), BF16),
                   jax.ShapeDtypeStruct((m, D), BF16), jax.ShapeDtypeStruct((m, D), BF16),
                   jax.ShapeDtypeStruct((m, DKT), F32)],
        compiler_params=_params(("parallel",)),
        name="gla_proj",
    )(x, nw, w_all, wg, wup, b_gate.reshape(1, DKT))

    L = min(GLA_CHUNK, s)
    levels = L.bit_length() - 1
    nc = s // L
    lvl = _gla_level_matrix(L, levels)
    lvl = jnp.asarray(np.concatenate([lvl, lvl], axis=1), dtype=BF16)
    kblk = pl.BlockSpec((L, DKT), lambda i, j: (i * nc + j, 0))
    vblk = pl.BlockSpec((L, D), lambda i, j: (i * nc + j, 0))
    og = pl.pallas_call(
        _gla_core_kernel,
        grid=(b, nc),
        in_specs=[kblk, kblk, vblk, vblk, kblk, _resident(((levels + 1) * L, 2 * L)), _resident((1, D))],
        out_specs=vblk,
        out_shape=jax.ShapeDtypeStruct((m, D), BF16),
        scratch_shapes=[pltpu.VMEM((H, GLA_DV, GLA_DK), F32)],
        compiler_params=_params(("parallel", "arbitrary")),
        name="gla_core",
    )(q, k, v, sr, la, lvl, norm_w.reshape(1, D))
    return og


def kernel(x, norm_w, ffn_w_in, ffn_w_out, mlstm_w_in, mlstm_b_gate, mlstm_conv_w, mlstm_norm_w, mlstm_w_out,
           fox_w_in, fox_b_f, fox_qk_norm, fox_w_out, gla_w_in, gla_w_gate_up, gla_b_gate, gla_norm_w,
           gla_w_out, final_norm_w):
    b, s, d = x.shape
    depth = norm_w.shape[0]
    xf = x.reshape(b * s, d)
    fw = final_norm_w.reshape(1, d)
    ffn_in, ffn_out = ffn_w_in.astype(BF16), ffn_w_out.astype(BF16)
    mlstm_in, mlstm_out = mlstm_w_in.astype(BF16), mlstm_w_out.astype(BF16)
    fox_in, fox_out = fox_w_in.astype(BF16), fox_w_out.astype(BF16)
    gla_in, gla_out = gla_w_in.astype(BF16), gla_w_out.astype(BF16)
    for layer in range(depth):
        kind, j = layer % N_MIXERS, layer // N_MIXERS
        xf = _ffn(xf, norm_w[layer, 0].reshape(1, d), ffn_in, ffn_out, (layer, 0), fw, False)
        nw = norm_w[layer, 1].reshape(1, d)
        if kind == 0:
            a = _mlstm_mixer(xf, b, s, nw, mlstm_in, j, mlstm_w_in[j], mlstm_b_gate[j], mlstm_conv_w[j],
                             mlstm_norm_w[j])
            mix = (a, mlstm_out, j)
        elif kind == 1:
            a = _fox_mixer(xf, b, s, nw, fox_in, j, fox_w_in[j], fox_b_f[j], fox_qk_norm[j])
            mix = (a, fox_out, j)
        else:
            a = _gla_mixer(xf, b, s, nw, gla_in, j, gla_w_in[j], gla_w_gate_up[j], gla_b_gate[j], gla_norm_w[j])
            mix = (a, gla_out, j)
        xf = _ffn(xf, norm_w[layer, 2].reshape(1, d), ffn_in, ffn_out, (layer, 1), fw, layer == depth - 1, mix)
    return xf.reshape(b, s, d)
```

```python
import functools

import numpy as np
import jax
import jax.numpy as jnp
from jax import lax
from jax.experimental import pallas as pl
from jax.experimental.pallas import tpu as pltpu

F32 = jnp.float32
BF16 = jnp.bfloat16

EPS = 1e-6
D_MODEL = 1024
D_FF = 2816
MLSTM_HEADS = 4
MLSTM_DH = D_MODEL // MLSTM_HEADS
CONV_WIDTH = 4
FOX_HEADS = 8
FOX_DH = D_MODEL // FOX_HEADS
GLA_HEADS = 4
GLA_DK_TOTAL = D_MODEL // 2
GLA_DK = GLA_DK_TOTAL // GLA_HEADS
GLA_DV = D_MODEL // GLA_HEADS
GLA_RANK = 16
GLA_TAU = 16.0
N_MIXERS = 3

LANES = 128
SUBLANES = 8
VMEM_LIMIT_BYTES = 56 * 1024 * 1024

ROW_TILE = 512
FFN_ROW_TILE = 1024
MLSTM_CHUNK = 256
MLSTM_CONV_COLS = 512
MLSTM_ROW_TILE = 256
GLA_CHUNK = 128
ATTN_TILE = 512
ATTN_HEADS_PER_STEP = 4
LOG2E = 1.4426950408889634


def _dot(a, b):
    return jnp.dot(a, b, preferred_element_type=F32)


def _dot_nt(a, b):
    return lax.dot_general(a, b, (((1,), (1,)), ((), ())), preferred_element_type=F32)


def _dot_tn(a, b):
    return lax.dot_general(a, b, (((0,), (0,)), ((), ())), preferred_element_type=F32)


def _rms(x, w):
    ms = jnp.mean(x * x, axis=-1, keepdims=True)
    return x * lax.rsqrt(ms + EPS) * w


def _sigmoid(x):
    return 1.0 / (1.0 + jnp.exp(-x))


def _log_sigmoid(x):
    return jnp.minimum(x, 0.0) - jnp.log1p(jnp.exp(-jnp.abs(x)))


def _split3(x):
    hi = x.astype(BF16)
    r1 = x - hi.astype(F32)
    mid = r1.astype(BF16)
    lo = (r1 - mid.astype(F32)).astype(BF16)
    return hi, mid, lo


def _cumsum_rows(tril, x):
    hi, mid, lo = _split3(x)
    return _dot(tril, hi) + _dot(tril, mid) + _dot(tril, lo)


def _cumsum_lanes(x, triu):
    hi, mid, lo = _split3(x)
    return _dot(hi, triu) + _dot(mid, triu) + _dot(lo, triu)


def _tri(n):
    row = lax.broadcasted_iota(jnp.int32, (n, n), 0)
    col = lax.broadcasted_iota(jnp.int32, (n, n), 1)
    tril = jnp.where(row >= col, 1.0, 0.0).astype(BF16)
    triu = jnp.where(row <= col, 1.0, 0.0).astype(BF16)
    return row, col, tril, triu


def _resident(shape):
    zeros = (0,) * len(shape)
    return pl.BlockSpec(shape, lambda *_: zeros, pipeline_mode=pl.Buffered(1))


def _pick(shape, *lead):
    zeros = (0,) * len(shape)
    return pl.BlockSpec((None,) * len(lead) + tuple(shape), lambda *_: tuple(lead) + zeros,
                        pipeline_mode=pl.Buffered(1))


def _params(semantics):
    return pltpu.CompilerParams(dimension_semantics=semantics, vmem_limit_bytes=VMEM_LIMIT_BYTES)


def _row_tile(s):
    return min(ROW_TILE, s)


def _ffn_kernel(*refs, mixed, final):
    if mixed:
        x_ref, a_ref, wmix_ref, nw_ref, win_ref, wout_ref, fw_ref, o_ref = refs
        x = x_ref[...] + _dot(a_ref[...], wmix_ref[...])
    else:
        x_ref, nw_ref, win_ref, wout_ref, fw_ref, o_ref = refs
        x = x_ref[...]
    xn = _rms(x, nw_ref[...]).astype(BF16)
    g = _dot(xn, win_ref[:, :D_FF])
    u = _dot(xn, win_ref[:, D_FF:])
    a = (g * _sigmoid(g) * u).astype(BF16)
    y = x + 0.5 * _dot(a, wout_ref[...])
    if final:
        y = _rms(y, fw_ref[...])
    o_ref[...] = y


def _ffn(x, nw, w_in, w_out, sel, fw, final, mix=None):
    m = x.shape[0]
    tm = min(FFN_ROW_TILE, m)
    row = pl.BlockSpec((tm, D_MODEL), lambda i: (i, 0))
    weights = [_resident((1, D_MODEL)), _pick((D_MODEL, 2 * D_FF), *sel), _pick((D_FF, D_MODEL), *sel),
               _resident((1, D_MODEL))]
    if mix is None:
        in_specs, args = [row] + weights, (x, nw, w_in, w_out, fw)
    else:
        a, w_mix, j = mix
        in_specs = [row, row, _pick((D_MODEL, D_MODEL), j)] + weights
        args = (x, a, w_mix, nw, w_in, w_out, fw)
    return pl.pallas_call(
        functools.partial(_ffn_kernel, mixed=mix is not None, final=final),
        grid=(m // tm,),
        in_specs=in_specs,
        out_specs=row,
        out_shape=jax.ShapeDtypeStruct((m, D_MODEL), F32),
        compiler_params=_params(("parallel",)),
        name="ffn" + ("_mix" if mix is not None else "") + ("_final" if final else ""),
    )(*args)


def _mlstm_proj_kernel(x_ref, nw_ref, w_ref, wg_ref, wgt_ref, bcol_ref, brow_ref, cw_ref,
                       q_ref, k_ref, v_ref, so_ref, gcol_ref, grow_ref, carry_ref):
    tm = x_ref.shape[0]

    @pl.when(pl.program_id(1) == 0)
    def _():
        carry_ref[:SUBLANES, :] = jnp.zeros((SUBLANES, carry_ref.shape[1]), F32)

    xn = _rms(x_ref[...], nw_ref[...]).astype(BF16)
    for c in range(2 * D_MODEL // MLSTM_CONV_COLS):
        cs = slice(c * MLSTM_CONV_COLS, (c + 1) * MLSTM_CONV_COLS)
        carry_ref[SUBLANES:, cs] = _dot(xn, w_ref[:, cs])
        cw = cw_ref[:, cs]
        conv = carry_ref[SUBLANES:, cs] * cw[CONV_WIDTH - 1:CONV_WIDTH]
        for j in range(1, CONV_WIDTH):
            conv = conv + carry_ref[SUBLANES - j:SUBLANES - j + tm, cs] * cw[CONV_WIDTH - 1 - j:CONV_WIDTH - j]
        carry_ref[:SUBLANES, cs] = carry_ref[tm:, cs]
        act = conv * _sigmoid(conv)
        if c * MLSTM_CONV_COLS < D_MODEL:
            q_ref[:, cs] = (act * MLSTM_DH ** -0.5).astype(BF16)
        else:
            k_ref[:, slice(cs.start - D_MODEL, cs.stop - D_MODEL)] = act.astype(BF16)
    v_ref[...] = _dot(xn, w_ref[:, 2 * D_MODEL:3 * D_MODEL]).astype(BF16)
    so_ref[...] = _sigmoid(_dot(xn, w_ref[:, 3 * D_MODEL:4 * D_MODEL])).astype(BF16)
    gcol_ref[...] = _dot(xn, wg_ref[...]) + bcol_ref[...]
    grow_ref[...] = _dot_nt(wgt_ref[...], xn) + brow_ref[...]


def _select_lane(parts, sel):
    return _dot(parts[0], sel) + _dot(parts[1], sel) + _dot(parts[2], sel)


def _mlstm_core_kernel(q_ref, k_ref, v_ref, so_ref, gcol_ref, grow_ref, nw_ref, o_ref, cn_ref, m_ref):
    L = q_ref.shape[0]
    H, DH = MLSTM_HEADS, MLSTM_DH
    nl = L // LANES

    @pl.when(pl.program_id(1) == 0)
    def _():
        cn_ref[...] = jnp.zeros_like(cn_ref)
        m_ref[...] = jnp.zeros_like(m_ref)

    _, _, tril, triu = _tri(L)
    row = lax.broadcasted_iota(jnp.int32, (L, LANES), 0)
    lane = lax.broadcasted_iota(jnp.int32, (L, LANES), 1)
    sel_row = lax.broadcasted_iota(jnp.int32, (LANES, LANES), 0)
    ones = jnp.ones((L, LANES), BF16)
    gcol = gcol_ref[...]
    grow = grow_ref[...]
    bc_all = _cumsum_rows(tril, _log_sigmoid(gcol))
    br_all = _cumsum_lanes(_log_sigmoid(grow), triu)
    gcol_parts = _split3(gcol)
    bc_parts = _split3(bc_all)

    for h in range(H):
        hs = slice(h * DH, (h + 1) * DH)
        q = q_ref[:, hs]
        k = k_ref[:, hs]
        v_aug = jnp.concatenate([v_ref[:, hs], ones], axis=1)
        icb = _select_lane(gcol_parts, jnp.where(sel_row == h, 1.0, 0.0).astype(BF16))
        bcb = _select_lane(bc_parts, jnp.where(sel_row == H + h, 1.0, 0.0).astype(BF16))
        a_row = grow[h:h + 1, :] - br_all[H + h:H + h + 1, :]
        m_prev = m_ref[h]
        cn_prev = cn_ref[h]

        a_tiles = [jnp.where(row >= lane + j * LANES, a_row[:, j * LANES:(j + 1) * LANES], -jnp.inf)
                   for j in range(nl)]
        a_max = a_tiles[0]
        for j in range(1, nl):
            a_max = jnp.maximum(a_max, a_tiles[j])
        u = jnp.maximum(m_prev, jnp.max(a_max, axis=-1, keepdims=True))
        inter = jnp.exp(m_prev - u)
        s = _dot_nt(q, k)
        sc = jnp.concatenate([s[:, j * LANES:(j + 1) * LANES] * jnp.exp(a_tiles[j] - u) for j in range(nl)],
                             axis=1)
        tot = _dot(sc.astype(BF16), v_aug) + jnp.concatenate([inter] * (DH // LANES + 1), axis=1) * _dot(
            q, cn_prev.astype(BF16))
        den = tot[:, DH:]
        r = 1.0 / jnp.maximum(jnp.abs(den), jnp.exp(-(bcb + u)))
        h_out = tot[:, :DH] * jnp.concatenate([r] * (DH // LANES), axis=1)

        b_last = bcb[L - 1:L, :]
        m_new = b_last + jnp.maximum(m_prev, jnp.max(a_row, axis=-1, keepdims=True))
        decay = jnp.exp(b_last + m_prev - m_new)
        w = jnp.exp(b_last - bcb + icb - m_new)
        kw = (k.astype(F32) * jnp.concatenate([w] * (DH // LANES), axis=1)).astype(BF16)
        cn_ref[h] = jnp.concatenate([decay] * (DH // LANES + 1), axis=1) * cn_prev + _dot_tn(kw, v_aug)
        m_ref[h] = m_new

        hn = _rms(h_out, nw_ref[:, hs])
        o_ref[:, hs] = (hn * so_ref[:, hs].astype(F32)).astype(BF16)


def _mlstm_mixer(x, b, s, nw, w_all, jm, w_in, b_gate, conv_w, norm_w):
    m = b * s
    D, H = D_MODEL, MLSTM_HEADS
    tm = min(MLSTM_ROW_TILE, s)
    nt = s // tm
    wgate = w_in[:, 4 * D:].astype(BF16)
    wg = jnp.pad(wgate, ((0, 0), (0, LANES - 2 * H)))
    wgt = wgate.T
    bcol = jnp.pad(b_gate, (0, LANES - 2 * H)).reshape(1, LANES)
    brow = b_gate.reshape(2 * H, 1)

    row = pl.BlockSpec((tm, D), lambda i, j: (i * nt + j, 0))
    q, k, v, so, gcol, grow = pl.pallas_call(
        _mlstm_proj_kernel,
        grid=(b, nt),
        in_specs=[row, _resident((1, D)), _pick((D, 4 * D + 2 * H), jm),
                  _resident((D, LANES)), _resident((2 * H, D)), _resident((1, LANES)), _resident((2 * H, 1)),
                  _resident((CONV_WIDTH, 2 * D))],
        out_specs=[row, row, row, row,
                   pl.BlockSpec((tm, LANES), lambda i, j: (i * nt + j, 0)),
                   pl.BlockSpec((2 * H, tm), lambda i, j: (0, i * nt + j))],
        out_shape=[jax.ShapeDtypeStruct((m, D), BF16)] * 4 + [
            jax.ShapeDtypeStruct((m, LANES), F32), jax.ShapeDtypeStruct((2 * H, m), F32)],
        scratch_shapes=[pltpu.VMEM((SUBLANES + tm, 2 * D), F32)],
        compiler_params=_params(("parallel", "arbitrary")),
        name="mlstm_proj",
    )(x, nw, w_all, wg, wgt, bcol, brow, conv_w)

    L = min(MLSTM_CHUNK, s)
    nc = s // L
    blk = pl.BlockSpec((L, D), lambda i, j: (i * nc + j, 0))
    hg = pl.pallas_call(
        _mlstm_core_kernel,
        grid=(b, nc),
        in_specs=[blk, blk, blk, blk,
                  pl.BlockSpec((L, LANES), lambda i, j: (i * nc + j, 0)),
                  pl.BlockSpec((2 * H, L), lambda i, j: (0, i * nc + j)),
                  _resident((1, D))],
        out_specs=blk,
        out_shape=jax.ShapeDtypeStruct((m, D), BF16),
        scratch_shapes=[pltpu.VMEM((H, MLSTM_DH, MLSTM_DH + LANES), F32), pltpu.VMEM((H, 1, LANES), F32)],
        compiler_params=_params(("parallel", "arbitrary")),
        name="mlstm_core",
    )(q, k, v, so, gcol, grow, norm_w.reshape(1, D))
    return hg


def _fox_proj_kernel(x_ref, nw_ref, w_ref, wf_ref, wft_ref, bcol_ref, brow_ref, qkn_ref,
                     q_ref, k_ref, v_ref, so_ref, fcol_ref, frow_ref, ccol_ref, crow_ref):
    tm = x_ref.shape[0]

    @pl.when(pl.program_id(1) == 0)
    def _():
        ccol_ref[...] = jnp.zeros_like(ccol_ref)
        crow_ref[...] = jnp.zeros_like(crow_ref)

    xn = _rms(x_ref[...], nw_ref[...]).astype(BF16)
    q = _dot(xn, w_ref[:, :D_MODEL])
    k = _dot(xn, w_ref[:, D_MODEL:2 * D_MODEL])
    qkn = qkn_ref[...]
    for h in range(FOX_HEADS):
        hs = slice(h * FOX_DH, (h + 1) * FOX_DH)
        q_ref[:, hs] = (_rms(q[:, hs], qkn[0:1]) * (FOX_DH ** -0.5 * LOG2E)).astype(BF16)
        k_ref[:, hs] = _rms(k[:, hs], qkn[1:2]).astype(BF16)
    v_ref[...] = _dot(xn, w_ref[:, 2 * D_MODEL:3 * D_MODEL]).astype(BF16)
    so_ref[...] = _sigmoid(_dot(xn, w_ref[:, 3 * D_MODEL:4 * D_MODEL])).astype(BF16)

    _, _, tril, triu = _tri(tm)
    lf_col = _log_sigmoid(_dot(xn, wf_ref[...]) + bcol_ref[...])
    lf_row = _log_sigmoid(_dot_nt(wft_ref[...], xn) + brow_ref[...])
    f_col = _cumsum_rows(tril, lf_col) + ccol_ref[...]
    f_row = _cumsum_lanes(lf_row, triu) + crow_ref[...]
    fcol_ref[...] = f_col * LOG2E
    for h in range(FOX_HEADS):
        frow_ref[h] = f_row[h:h + 1, :] * LOG2E
    ccol_ref[...] = f_col[tm - 1:tm, :]
    crow_ref[...] = f_row[:, tm - 1:tm]


def _fox_attn_kernel(q_ref, k_ref, v_ref, so_ref, fcol_ref, frow_ref, o_ref,
                     s_ref, m_ref, l_ref, acc_ref, fq_ref):
    t = q_ref.shape[0]
    G = s_ref.shape[0]
    nl = t // LANES
    hp = pl.program_id(1)
    qi = pl.program_id(2)

    lane = lax.broadcasted_iota(jnp.int32, (t, LANES), 1)
    row = lax.broadcasted_iota(jnp.int32, (t, LANES), 0)
    for g in range(G):
        fq = jnp.sum(jnp.where(lane == hp * G + g, fcol_ref[...], 0.0), axis=-1, keepdims=True)
        fq_ref[g] = jnp.broadcast_to(fq, (t, LANES))
    m_ref[...] = jnp.full_like(m_ref, -jnp.inf)
    l_ref[...] = jnp.zeros_like(l_ref)
    acc_ref[...] = jnp.zeros_like(acc_ref)

    def scores(kt, diagonal):
        off = pl.multiple_of(kt * t, t)
        for g in range(G):
            hs = slice(g * FOX_DH, (g + 1) * FOX_DH)
            s = _dot_nt(q_ref[:, hs], k_ref[pl.ds(off, t), hs])
            fk = frow_ref[g, :, pl.ds(off, t)]
            fqb = fq_ref[g]
            m = m_ref[g]
            for j in range(nl):
                ls = slice(j * LANES, (j + 1) * LANES)
                sj = s[:, ls] + (fqb - fk[:, ls])
                if diagonal:
                    sj = jnp.where(row >= lane + j * LANES, sj, -jnp.inf)
                s_ref[g, kt, :, ls] = sj
                m = jnp.maximum(m, sj)
            m_ref[g] = m

    def scores_body(kt, carry):
        scores(kt, False)
        return carry

    lax.fori_loop(0, qi, scores_body, 0)
    scores(qi, True)
    for g in range(G):
        m_ref[g] = jnp.broadcast_to(jnp.max(m_ref[g], axis=-1, keepdims=True), (t, LANES))

    def accumulate(kt, carry):
        off = pl.multiple_of(kt * t, t)
        for g in range(G):
            hs = slice(g * FOX_DH, (g + 1) * FOX_DH)
            mb = m_ref[g]
            l = l_ref[g]
            ps = []
            for j in range(nl):
                p = jnp.exp2(s_ref[g, kt, :, j * LANES:(j + 1) * LANES] - mb)
                l = l + p
                ps.append(p.astype(BF16))
            l_ref[g] = l
            acc_ref[g] += _dot(jnp.concatenate(ps, axis=1), v_ref[pl.ds(off, t), hs])
        return carry

    lax.fori_loop(0, qi + 1, accumulate, 0)
    for g in range(G):
        hs = slice(g * FOX_DH, (g + 1) * FOX_DH)
        l = jnp.sum(l_ref[g], axis=-1, keepdims=True)
        o_ref[:, hs] = (acc_ref[g] * (1.0 / l) * so_ref[:, hs].astype(F32)).astype(BF16)


def _fox_mixer(x, b, s, nw, w_all, jm, w_in, b_f, qk_norm):
    m = b * s
    D, H, DH = D_MODEL, FOX_HEADS, FOX_DH
    tm = _row_tile(s)
    nt = s // tm
    wgate = w_in[:, 4 * D:].astype(BF16)
    wf = jnp.pad(wgate, ((0, 0), (0, LANES - H)))
    wft = wgate.T
    bcol = jnp.pad(b_f, (0, LANES - H)).reshape(1, LANES)
    brow = b_f.reshape(H, 1)

    row = pl.BlockSpec((tm, D), lambda i, j: (i * nt + j, 0))
    q, k, v, so, fcol, frow = pl.pallas_call(
        _fox_proj_kernel,
        grid=(b, nt),
        in_specs=[row, _resident((1, D)), _pick((D, 4 * D + H), jm), _resident((D, LANES)), _resident((H, D)),
                  _resident((1, LANES)), _resident((H, 1)), _resident((2, DH))],
        out_specs=[row, row, row, row,
                   pl.BlockSpec((tm, LANES), lambda i, j: (i * nt + j, 0)),
                   pl.BlockSpec((H, 1, tm), lambda i, j: (0, 0, i * nt + j))],
        out_shape=[jax.ShapeDtypeStruct((m, D), BF16)] * 4 + [
            jax.ShapeDtypeStruct((m, LANES), F32), jax.ShapeDtypeStruct((H, 1, m), F32)],
        scratch_shapes=[pltpu.VMEM((1, LANES), F32), pltpu.VMEM((H, 1), F32)],
        compiler_params=_params(("parallel", "arbitrary")),
        name="fox_proj",
    )(x, nw, w_all, wf, wft, bcol, brow, qk_norm)

    t = min(ATTN_TILE, s)
    nq = s // t
    G = ATTN_HEADS_PER_STEP
    qblk = pl.BlockSpec((t, G * DH), lambda bi, hi, qi: (bi * nq + qi, hi))
    kblk = pl.BlockSpec((s, G * DH), lambda bi, hi, qi: (bi, hi), pipeline_mode=pl.Buffered(1))
    att = pl.pallas_call(
        _fox_attn_kernel,
        grid=(b, H // G, nq),
        in_specs=[qblk, kblk, kblk, qblk,
                  pl.BlockSpec((t, LANES), lambda bi, hi, qi: (bi * nq + qi, 0)),
                  pl.BlockSpec((G, 1, s), lambda bi, hi, qi: (hi, 0, bi))],
        out_specs=qblk,
        out_shape=jax.ShapeDtypeStruct((m, D), BF16),
        scratch_shapes=[pltpu.VMEM((G, nq, t, t), F32), pltpu.VMEM((G, t, LANES), F32),
                        pltpu.VMEM((G, t, LANES), F32), pltpu.VMEM((G, t, DH), F32),
                        pltpu.VMEM((G, t, LANES), F32)],
        compiler_params=_params(("parallel", "parallel", "arbitrary")),
        name="fox_attn",
    )(q, k, v, so, fcol, frow)
    return att


def _gla_level_matrix(L, levels):
    t = np.arange(L)[:, None]
    c = np.arange(L)[None, :]
    blocks = [(c <= t)]
    for j in range(levels):
        hs = 1 << j
        mid = (t & ~(2 * hs - 1)) + hs
        upper = (t & hs) != 0
        blocks.append(np.where(upper, (c > mid) & (c <= t), (c > t) & (c <= mid)))
    return np.concatenate(blocks, axis=0).astype(np.float32)


def _gla_proj_kernel(x_ref, nw_ref, w_ref, wg_ref, wup_ref, bg_ref,
                     q_ref, k_ref, v_ref, sr_ref, la_ref):
    xn = _rms(x_ref[...], nw_ref[...]).astype(BF16)
    DKT = GLA_DK_TOTAL
    q_ref[...] = (_dot(xn, w_ref[:, :DKT]) * GLA_DK ** -0.5).astype(BF16)
    k_ref[...] = _dot(xn, w_ref[:, DKT:2 * DKT]).astype(BF16)
    v_ref[...] = _dot(xn, w_ref[:, 2 * DKT:2 * DKT + D_MODEL]).astype(BF16)
    r = _dot(xn, w_ref[:, 2 * DKT + D_MODEL:2 * DKT + 2 * D_MODEL])
    sr_ref[...] = (r * _sigmoid(r)).astype(BF16)
    glr = _dot(xn, wg_ref[...])
    z = _dot(glr.astype(BF16), wup_ref[...]) + bg_ref[...]
    la_ref[...] = _log_sigmoid(z) * (1.0 / GLA_TAU)


def _gla_core_kernel(q_ref, k_ref, v_ref, sr_ref, la_ref, lvl_ref, nw_ref, o_ref, st_ref):
    L = q_ref.shape[0]
    H, DK, DV = GLA_HEADS, GLA_DK, GLA_DV
    levels = lvl_ref.shape[0] // L - 1

    @pl.when(pl.program_id(1) == 0)
    def _():
        st_ref[...] = jnp.zeros_like(st_ref)

    hi, mid, _ = _split3(la_ref[...])
    args = _dot(lvl_ref[...], jnp.concatenate([hi, mid], axis=0))
    row = lax.broadcasted_iota(jnp.int32, (L, L), 0)
    col = lax.broadcasted_iota(jnp.int32, (L, L), 1)
    diff = jnp.where(row > col, row ^ col, 0)

    for h in range(H):
        ks = slice(h * DK, (h + 1) * DK)
        vs = slice(h * DV, (h + 1) * DV)
        q = q_ref[:, ks].astype(F32)
        k = k_ref[:, ks].astype(F32)
        v = v_ref[:, vs]
        g = args[0:L, ks]
        a = jnp.where(row == col, jnp.sum(q * k, axis=-1, keepdims=True), 0.0)
        for j in range(levels):
            e = jnp.exp(args[(j + 1) * L:(j + 2) * L, ks])
            aj = _dot_nt((q * e).astype(BF16), (k * e).astype(BF16))
            a = jnp.where((diff >> j) == 1, aj, a)
        st = st_ref[h]
        o = _dot(a.astype(BF16), v) + _dot_nt((q * jnp.exp(g)).astype(BF16), st.astype(BF16))
        g_last = g[L - 1:L, :]
        kd = (k * jnp.exp(g_last - g)).astype(BF16)
        st_ref[h] = st * jnp.exp(g_last) + _dot_tn(v, kd)
        on = _rms(o, nw_ref[:, vs])
        o_ref[:, vs] = (on * sr_ref[:, vs].astype(F32)).astype(BF16)


def _gla_mixer(x, b, s, nw, w_all, jm, w_in, w_gate_up, b_gate, norm_w):
    m = b * s
    D, H, DKT, R = D_MODEL, GLA_HEADS, GLA_DK_TOTAL, GLA_RANK
    tm = _row_tile(m)
    wg = jnp.pad(w_in[:, 2 * DKT + 2 * D:].astype(BF16), ((0, 0), (0, LANES - R)))
    wup = jnp.pad(w_gate_up.astype(BF16), ((0, LANES - R), (0, 0)))

    row = pl.BlockSpec((tm, D), lambda i: (i, 0))
    half = pl.BlockSpec((tm, DKT), lambda i: (i, 0))
    q, k, v, sr, la = pl.pallas_call(
        _gla_proj_kernel,
        grid=(m // tm,),
        in_specs=[row, _resident((1, D)), _pick((D, 2 * DKT + 2 * D + R), jm), _resident((D, LANES)),
                  _resident((LANES, DKT)), _resident((1, DKT))],
        out_specs=[half, half, row, row, half],
        out_shape=[jax.ShapeDtypeStruct((m, DKT), BF16), jax.ShapeDtypeStruct((m, DKT), BF16),
                   jax.ShapeDtypeStruct((m, D), BF16), jax.ShapeDtypeStruct((m, D), BF16),
                   jax.ShapeDtypeStruct((m, DKT), F32)],
        compiler_params=_params(("parallel",)),
        name="gla_proj",
    )(x, nw, w_all, wg, wup, b_gate.reshape(1, DKT))

    L = min(GLA_CHUNK, s)
    levels = L.bit_length() - 1
    nc = s // L
    lvl = _gla_level_matrix(L, levels)
    lvl = jnp.asarray(np.concatenate([lvl, lvl], axis=1), dtype=BF16)
    kblk = pl.BlockSpec((L, DKT), lambda i, j: (i * nc + j, 0))
    vblk = pl.BlockSpec((L, D), lambda i, j: (i * nc + j, 0))
    og = pl.pallas_call(
        _gla_core_kernel,
        grid=(b, nc),
        in_specs=[kblk, kblk, vblk, vblk, kblk, _resident(((levels + 1) * L, 2 * L)), _resident((1, D))],
        out_specs=vblk,
        out_shape=jax.ShapeDtypeStruct((m, D), BF16),
        scratch_shapes=[pltpu.VMEM((H, GLA_DV, GLA_DK), F32)],
        compiler_params=_params(("parallel", "arbitrary")),
        name="gla_core",
    )(q, k, v, sr, la, lvl, norm_w.reshape(1, D))
    return og


def kernel(x, norm_w, ffn_w_in, ffn_w_out, mlstm_w_in, mlstm_b_gate, mlstm_conv_w, mlstm_norm_w, mlstm_w_out,
           fox_w_in, fox_b_f, fox_qk_norm, fox_w_out, gla_w_in, gla_w_gate_up, gla_b_gate, gla_norm_w,
           gla_w_out, final_norm_w):
    b, s, d = x.shape
    depth = norm_w.shape[0]
    xf = x.reshape(b * s, d)
    fw = final_norm_w.reshape(1, d)
    ffn_in, ffn_out = ffn_w_in.astype(BF16), ffn_w_out.astype(BF16)
    mlstm_in, mlstm_out = mlstm_w_in.astype(BF16), mlstm_w_out.astype(BF16)
    fox_in, fox_out = fox_w_in.astype(BF16), fox_w_out.astype(BF16)
    gla_in, gla_out = gla_w_in.astype(BF16), gla_w_out.astype(BF16)
    for layer in range(depth):
        kind, j = layer % N_MIXERS, layer // N_MIXERS
        xf = _ffn(xf, norm_w[layer, 0].reshape(1, d), ffn_in, ffn_out, (layer, 0), fw, False)
        nw = norm_w[layer, 1].reshape(1, d)
        if kind == 0:
            a = _mlstm_mixer(xf, b, s, nw, mlstm_in, j, mlstm_w_in[j], mlstm_b_gate[j], mlstm_conv_w[j],
                             mlstm_norm_w[j])
            mix = (a, mlstm_out, j)
        elif kind == 1:
            a = _fox_mixer(xf, b, s, nw, fox_in, j, fox_w_in[j], fox_b_f[j], fox_qk_norm[j])
            mix = (a, fox_out, j)
        else:
            a = _gla_mixer(xf, b, s, nw, gla_in, j, gla_w_in[j], gla_w_gate_up[j], gla_b_gate[j], gla_norm_w[j])
            mix = (a, gla_out, j)
        xf = _ffn(xf, norm_w[layer, 2].reshape(1, d), ffn_in, ffn_out, (layer, 1), fw, layer == depth - 1, mix)
    return xf.reshape(b, s, d)
```

```python
import functools

import numpy as np
import jax
import jax.numpy as jnp
from jax import lax
from jax.experimental import pallas as pl
from jax.experimental.pallas import tpu as pltpu

F32 = jnp.float32
BF16 = jnp.bfloat16

EPS = 1e-6
D_MODEL = 1024
D_FF = 2816
MLSTM_HEADS = 4
MLSTM_DH = D_MODEL // MLSTM_HEADS
CONV_WIDTH = 4
FOX_HEADS = 8
FOX_DH = D_MODEL // FOX_HEADS
GLA_HEADS = 4
GLA_DK_TOTAL = D_MODEL // 2
GLA_DK = GLA_DK_TOTAL // GLA_HEADS
GLA_DV = D_MODEL // GLA_HEADS
GLA_RANK = 16
GLA_TAU = 16.0
N_MIXERS = 3

LANES = 128
SUBLANES = 8
VMEM_LIMIT_BYTES = 56 * 1024 * 1024

ROW_TILE = 512
FFN_ROW_TILE = 1024
MLSTM_CHUNK = 256
MLSTM_CONV_COLS = 512
MLSTM_ROW_TILE = 256
GLA_CHUNK = 128
GLA_ROW_TILE = 1024
ATTN_TILE = 512
ATTN_HEADS_PER_STEP = 4
LOG2E = 1.4426950408889634


def _dot(a, b):
    return jnp.dot(a, b, preferred_element_type=F32)


def _dot_nt(a, b):
    return lax.dot_general(a, b, (((1,), (1,)), ((), ())), preferred_element_type=F32)


def _dot_tn(a, b):
    return lax.dot_general(a, b, (((0,), (0,)), ((), ())), preferred_element_type=F32)


def _rms(x, w):
    ms = jnp.mean(x * x, axis=-1, keepdims=True)
    return x * lax.rsqrt(ms + EPS) * w


def _sigmoid(x):
    return 1.0 / (1.0 + jnp.exp(-x))


def _log_sigmoid(x):
    return jnp.minimum(x, 0.0) - jnp.log(1.0 + jnp.exp(-jnp.abs(x)))


def _split3(x):
    hi = x.astype(BF16)
    r1 = x - hi.astype(F32)
    mid = r1.astype(BF16)
    lo = (r1 - mid.astype(F32)).astype(BF16)
    return hi, mid, lo


def _cumsum_rows(tril, x):
    hi, mid, lo = _split3(x)
    return _dot(tril, hi) + _dot(tril, mid) + _dot(tril, lo)


def _cumsum_lanes(x, triu):
    hi, mid, lo = _split3(x)
    return _dot(hi, triu) + _dot(mid, triu) + _dot(lo, triu)


def _tri(n):
    row = lax.broadcasted_iota(jnp.int32, (n, n), 0)
    col = lax.broadcasted_iota(jnp.int32, (n, n), 1)
    tril = jnp.where(row >= col, 1.0, 0.0).astype(BF16)
    triu = jnp.where(row <= col, 1.0, 0.0).astype(BF16)
    return row, col, tril, triu


def _resident(shape):
    zeros = (0,) * len(shape)
    return pl.BlockSpec(shape, lambda *_: zeros, pipeline_mode=pl.Buffered(1))


def _pick(shape, *lead):
    zeros = (0,) * len(shape)
    return pl.BlockSpec((None,) * len(lead) + tuple(shape), lambda *_: tuple(lead) + zeros,
                        pipeline_mode=pl.Buffered(1))


def _params(semantics):
    return pltpu.CompilerParams(dimension_semantics=semantics, vmem_limit_bytes=VMEM_LIMIT_BYTES)


def _row_tile(s):
    return min(ROW_TILE, s)


def _ffn_kernel(*refs, mixed, final):
    if mixed:
        x_ref, a_ref, wmix_ref, nw_ref, win_ref, wout_ref, fw_ref, o_ref = refs
        x = x_ref[...] + _dot(a_ref[...], wmix_ref[...])
    else:
        x_ref, nw_ref, win_ref, wout_ref, fw_ref, o_ref = refs
        x = x_ref[...]
    xn = _rms(x, nw_ref[...]).astype(BF16)
    g = _dot(xn, win_ref[:, :D_FF])
    u = _dot(xn, win_ref[:, D_FF:])
    a = (g * _sigmoid(g) * u).astype(BF16)
    y = x + 0.5 * _dot(a, wout_ref[...])
    if final:
        y = _rms(y, fw_ref[...])
    o_ref[...] = y


def _ffn(x, nw, w_in, w_out, sel, fw, final, mix=None):
    m = x.shape[0]
    tm = min(FFN_ROW_TILE, m)
    row = pl.BlockSpec((tm, D_MODEL), lambda i: (i, 0))
    weights = [_resident((1, D_MODEL)), _pick((D_MODEL, 2 * D_FF), *sel), _pick((D_FF, D_MODEL), *sel),
               _resident((1, D_MODEL))]
    if mix is None:
        in_specs, args = [row] + weights, (x, nw, w_in, w_out, fw)
    else:
        a, w_mix, j = mix
        in_specs = [row, row, _pick((D_MODEL, D_MODEL), j)] + weights
        args = (x, a, w_mix, nw, w_in, w_out, fw)
    return pl.pallas_call(
        functools.partial(_ffn_kernel, mixed=mix is not None, final=final),
        grid=(m // tm,),
        in_specs=in_specs,
        out_specs=row,
        out_shape=jax.ShapeDtypeStruct((m, D_MODEL), F32),
        compiler_params=_params(("parallel",)),
        name="ffn" + ("_mix" if mix is not None else "") + ("_final" if final else ""),
    )(*args)


def _mlstm_proj_kernel(x_ref, nw_ref, w_ref, wg_ref, wgt_ref, bcol_ref, brow_ref, cw_ref,
                       q_ref, k_ref, v_ref, so_ref, gcol_ref, grow_ref, carry_ref):
    tm = x_ref.shape[0]

    @pl.when(pl.program_id(1) == 0)
    def _():
        carry_ref[:SUBLANES, :] = jnp.zeros((SUBLANES, carry_ref.shape[1]), F32)

    xn = _rms(x_ref[...], nw_ref[...]).astype(BF16)
    for c in range(2 * D_MODEL // MLSTM_CONV_COLS):
        cs = slice(c * MLSTM_CONV_COLS, (c + 1) * MLSTM_CONV_COLS)
        carry_ref[SUBLANES:, cs] = _dot(xn, w_ref[:, cs])
        cw = cw_ref[:, cs]
        conv = carry_ref[SUBLANES:, cs] * cw[CONV_WIDTH - 1:CONV_WIDTH]
        for j in range(1, CONV_WIDTH):
            conv = conv + carry_ref[SUBLANES - j:SUBLANES - j + tm, cs] * cw[CONV_WIDTH - 1 - j:CONV_WIDTH - j]
        carry_ref[:SUBLANES, cs] = carry_ref[tm:, cs]
        act = conv * _sigmoid(conv)
        if c * MLSTM_CONV_COLS < D_MODEL:
            q_ref[:, cs] = (act * MLSTM_DH ** -0.5).astype(BF16)
        else:
            k_ref[:, slice(cs.start - D_MODEL, cs.stop - D_MODEL)] = act.astype(BF16)
    v_ref[...] = _dot(xn, w_ref[:, 2 * D_MODEL:3 * D_MODEL]).astype(BF16)
    so_ref[...] = _sigmoid(_dot(xn, w_ref[:, 3 * D_MODEL:4 * D_MODEL])).astype(BF16)
    gcol_ref[...] = _dot(xn, wg_ref[...]) + bcol_ref[...]
    grow_ref[...] = _dot_nt(wgt_ref[...], xn) + brow_ref[...]


def _select_lane(parts, sel):
    return _dot(parts[0], sel) + _dot(parts[1], sel) + _dot(parts[2], sel)


def _mlstm_core_kernel(q_ref, k_ref, v_ref, so_ref, gcol_ref, grow_ref, nw_ref, o_ref, cn_ref, m_ref):
    L = q_ref.shape[0]
    H, DH = MLSTM_HEADS, MLSTM_DH
    nl = L // LANES

    @pl.when(pl.program_id(1) == 0)
    def _():
        cn_ref[...] = jnp.zeros_like(cn_ref)
        m_ref[...] = jnp.zeros_like(m_ref)

    _, _, tril, triu = _tri(L)
    row = lax.broadcasted_iota(jnp.int32, (L, LANES), 0)
    lane = lax.broadcasted_iota(jnp.int32, (L, LANES), 1)
    sel_row = lax.broadcasted_iota(jnp.int32, (LANES, LANES), 0)
    ones = jnp.ones((L, LANES), BF16)
    gcol = gcol_ref[...]
    grow = grow_ref[...]
    bc_all = _cumsum_rows(tril, _log_sigmoid(gcol))
    br_all = _cumsum_lanes(_log_sigmoid(grow), triu)
    gcol_parts = _split3(gcol)
    bc_parts = _split3(bc_all)

    for h in range(H):
        hs = slice(h * DH, (h + 1) * DH)
        q = q_ref[:, hs]
        k = k_ref[:, hs]
        v_aug = jnp.concatenate([v_ref[:, hs], ones], axis=1)
        icb = _select_lane(gcol_parts, jnp.where(sel_row == h, 1.0, 0.0).astype(BF16))
        bcb = _select_lane(bc_parts, jnp.where(sel_row == H + h, 1.0, 0.0).astype(BF16))
        a_row = grow[h:h + 1, :] - br_all[H + h:H + h + 1, :]
        m_prev = m_ref[h]
        cn_prev = cn_ref[h]

        a_tiles = [jnp.where(row >= lane + j * LANES, a_row[:, j * LANES:(j + 1) * LANES], -jnp.inf)
                   for j in range(nl)]
        a_max = a_tiles[0]
        for j in range(1, nl):
            a_max = jnp.maximum(a_max, a_tiles[j])
        u = jnp.maximum(m_prev, jnp.max(a_max, axis=-1, keepdims=True))
        inter = jnp.exp(m_prev - u)
        s = _dot_nt(q, k)
        sc = jnp.concatenate([s[:, j * LANES:(j + 1) * LANES] * jnp.exp(a_tiles[j] - u) for j in range(nl)],
                             axis=1)
        tot = _dot(sc.astype(BF16), v_aug) + jnp.concatenate([inter] * (DH // LANES + 1), axis=1) * _dot(
            q, cn_prev.astype(BF16))
        den = tot[:, DH:]
        r = 1.0 / jnp.maximum(jnp.abs(den), jnp.exp(-(bcb + u)))
        h_out = tot[:, :DH] * jnp.concatenate([r] * (DH // LANES), axis=1)

        b_last = bcb[L - 1:L, :]
        m_new = b_last + jnp.maximum(m_prev, jnp.max(a_row, axis=-1, keepdims=True))
        decay = jnp.exp(b_last + m_prev - m_new)
        w = jnp.exp(b_last - bcb + icb - m_new)
        kw = (k.astype(F32) * jnp.concatenate([w] * (DH // LANES), axis=1)).astype(BF16)
        cn_ref[h] = jnp.concatenate([decay] * (DH // LANES + 1), axis=1) * cn_prev + _dot_tn(kw, v_aug)
        m_ref[h] = m_new

        hn = _rms(h_out, nw_ref[:, hs])
        o_ref[:, hs] = (hn * so_ref[:, hs].astype(F32)).astype(BF16)


def _mlstm_mixer(x, b, s, nw, w_all, jm, w_in, b_gate, conv_w, norm_w):
    m = b * s
    D, H = D_MODEL, MLSTM_HEADS
    tm = min(MLSTM_ROW_TILE, s)
    nt = s // tm
    wgate = w_in[:, 4 * D:].astype(BF16)
    wg = jnp.pad(wgate, ((0, 0), (0, LANES - 2 * H)))
    wgt = wgate.T
    bcol = jnp.pad(b_gate, (0, LANES - 2 * H)).reshape(1, LANES)
    brow = b_gate.reshape(2 * H, 1)

    row = pl.BlockSpec((tm, D), lambda i, j: (i * nt + j, 0))
    q, k, v, so, gcol, grow = pl.pallas_call(
        _mlstm_proj_kernel,
        grid=(b, nt),
        in_specs=[row, _resident((1, D)), _pick((D, 4 * D + 2 * H), jm),
                  _resident((D, LANES)), _resident((2 * H, D)), _resident((1, LANES)), _resident((2 * H, 1)),
                  _resident((CONV_WIDTH, 2 * D))],
        out_specs=[row, row, row, row,
                   pl.BlockSpec((tm, LANES), lambda i, j: (i * nt + j, 0)),
                   pl.BlockSpec((2 * H, tm), lambda i, j: (0, i * nt + j))],
        out_shape=[jax.ShapeDtypeStruct((m, D), BF16)] * 4 + [
            jax.ShapeDtypeStruct((m, LANES), F32), jax.ShapeDtypeStruct((2 * H, m), F32)],
        scratch_shapes=[pltpu.VMEM((SUBLANES + tm, 2 * D), F32)],
        compiler_params=_params(("parallel", "arbitrary")),
        name="mlstm_proj",
    )(x, nw, w_all, wg, wgt, bcol, brow, conv_w)

    L = min(MLSTM_CHUNK, s)
    nc = s // L
    blk = pl.BlockSpec((L, D), lambda i, j: (i * nc + j, 0))
    hg = pl.pallas_call(
        _mlstm_core_kernel,
        grid=(b, nc),
        in_specs=[blk, blk, blk, blk,
                  pl.BlockSpec((L, LANES), lambda i, j: (i * nc + j, 0)),
                  pl.BlockSpec((2 * H, L), lambda i, j: (0, i * nc + j)),
                  _resident((1, D))],
        out_specs=blk,
        out_shape=jax.ShapeDtypeStruct((m, D), BF16),
        scratch_shapes=[pltpu.VMEM((H, MLSTM_DH, MLSTM_DH + LANES), F32), pltpu.VMEM((H, 1, LANES), F32)],
        compiler_params=_params(("parallel", "arbitrary")),
        name="mlstm_core",
    )(q, k, v, so, gcol, grow, norm_w.reshape(1, D))
    return hg


def _fox_proj_kernel(x_ref, nw_ref, w_ref, wf_ref, wft_ref, bcol_ref, brow_ref, qkn_ref,
                     q_ref, k_ref, v_ref, so_ref, fcol_ref, frow_ref, ccol_ref, crow_ref):
    tm = x_ref.shape[0]

    @pl.when(pl.program_id(1) == 0)
    def _():
        ccol_ref[...] = jnp.zeros_like(ccol_ref)
        crow_ref[...] = jnp.zeros_like(crow_ref)

    xn = _rms(x_ref[...], nw_ref[...]).astype(BF16)
    q = _dot(xn, w_ref[:, :D_MODEL])
    k = _dot(xn, w_ref[:, D_MODEL:2 * D_MODEL])
    qkn = qkn_ref[...]
    for h in range(FOX_HEADS):
        hs = slice(h * FOX_DH, (h + 1) * FOX_DH)
        q_ref[:, hs] = (_rms(q[:, hs], qkn[0:1]) * (FOX_DH ** -0.5 * LOG2E)).astype(BF16)
        k_ref[:, hs] = _rms(k[:, hs], qkn[1:2]).astype(BF16)
    v_ref[...] = _dot(xn, w_ref[:, 2 * D_MODEL:3 * D_MODEL]).astype(BF16)
    so_ref[...] = _sigmoid(_dot(xn, w_ref[:, 3 * D_MODEL:4 * D_MODEL])).astype(BF16)

    _, _, tril, triu = _tri(tm)
    lf_col = _log_sigmoid(_dot(xn, wf_ref[...]) + bcol_ref[...])
    lf_row = _log_sigmoid(_dot_nt(wft_ref[...], xn) + brow_ref[...])
    f_col = _cumsum_rows(tril, lf_col) + ccol_ref[...]
    f_row = _cumsum_lanes(lf_row, triu) + crow_ref[...]
    fcol_ref[...] = f_col * LOG2E
    for h in range(FOX_HEADS):
        frow_ref[h] = f_row[h:h + 1, :] * LOG2E
    ccol_ref[...] = f_col[tm - 1:tm, :]
    crow_ref[...] = f_row[:, tm - 1:tm]


def _fox_attn_kernel(q_ref, k_ref, v_ref, so_ref, fcol_ref, frow_ref, o_ref,
                     s_ref, m_ref, l_ref, acc_ref, fq_ref):
    t = q_ref.shape[0]
    G = s_ref.shape[0]
    nl = t // LANES
    hp = pl.program_id(1)
    qi = pl.program_id(2)

    lane = lax.broadcasted_iota(jnp.int32, (t, LANES), 1)
    row = lax.broadcasted_iota(jnp.int32, (t, LANES), 0)
    for g in range(G):
        fq = jnp.sum(jnp.where(lane == hp * G + g, fcol_ref[...], 0.0), axis=-1, keepdims=True)
        fq_ref[g] = jnp.broadcast_to(fq, (t, LANES))
    m_ref[...] = jnp.full_like(m_ref, -jnp.inf)
    l_ref[...] = jnp.zeros_like(l_ref)
    acc_ref[...] = jnp.zeros_like(acc_ref)

    def scores(kt, diagonal):
        off = pl.multiple_of(kt * t, t)
        for g in range(G):
            hs = slice(g * FOX_DH, (g + 1) * FOX_DH)
            s = _dot_nt(q_ref[:, hs], k_ref[pl.ds(off, t), hs])
            fk = frow_ref[g, :, pl.ds(off, t)]
            fqb = fq_ref[g]
            m = m_ref[g]
            for j in range(nl):
                ls = slice(j * LANES, (j + 1) * LANES)
                sj = s[:, ls] + (fqb - fk[:, ls])
                if diagonal:
                    sj = jnp.where(row >= lane + j * LANES, sj, -jnp.inf)
                s_ref[g, kt, :, ls] = sj
                m = jnp.maximum(m, sj)
            m_ref[g] = m

    def scores_body(kt, carry):
        scores(kt, False)
        return carry

    lax.fori_loop(0, qi, scores_body, 0)
    scores(qi, True)
    for g in range(G):
        m_ref[g] = jnp.broadcast_to(jnp.max(m_ref[g], axis=-1, keepdims=True), (t, LANES))

    def accumulate(kt, carry):
        off = pl.multiple_of(kt * t, t)
        for g in range(G):
            hs = slice(g * FOX_DH, (g + 1) * FOX_DH)
            mb = m_ref[g]
            l = l_ref[g]
            ps = []
            for j in range(nl):
                p = jnp.exp2(s_ref[g, kt, :, j * LANES:(j + 1) * LANES] - mb)
                l = l + p
                ps.append(p.astype(BF16))
            l_ref[g] = l
            acc_ref[g] += _dot(jnp.concatenate(ps, axis=1), v_ref[pl.ds(off, t), hs])
        return carry

    lax.fori_loop(0, qi + 1, accumulate, 0)
    for g in range(G):
        hs = slice(g * FOX_DH, (g + 1) * FOX_DH)
        l = jnp.sum(l_ref[g], axis=-1, keepdims=True)
        o_ref[:, hs] = (acc_ref[g] * (1.0 / l) * so_ref[:, hs].astype(F32)).astype(BF16)


def _fox_mixer(x, b, s, nw, w_all, jm, w_in, b_f, qk_norm):
    m = b * s
    D, H, DH = D_MODEL, FOX_HEADS, FOX_DH
    tm = _row_tile(s)
    nt = s // tm
    wgate = w_in[:, 4 * D:].astype(BF16)
    wf = jnp.pad(wgate, ((0, 0), (0, LANES - H)))
    wft = wgate.T
    bcol = jnp.pad(b_f, (0, LANES - H)).reshape(1, LANES)
    brow = b_f.reshape(H, 1)

    row = pl.BlockSpec((tm, D), lambda i, j: (i * nt + j, 0))
    q, k, v, so, fcol, frow = pl.pallas_call(
        _fox_proj_kernel,
        grid=(b, nt),
        in_specs=[row, _resident((1, D)), _pick((D, 4 * D + H), jm), _resident((D, LANES)), _resident((H, D)),
                  _resident((1, LANES)), _resident((H, 1)), _resident((2, DH))],
        out_specs=[row, row, row, row,
                   pl.BlockSpec((tm, LANES), lambda i, j: (i * nt + j, 0)),
                   pl.BlockSpec((H, 1, tm), lambda i, j: (0, 0, i * nt + j))],
        out_shape=[jax.ShapeDtypeStruct((m, D), BF16)] * 4 + [
            jax.ShapeDtypeStruct((m, LANES), F32), jax.ShapeDtypeStruct((H, 1, m), F32)],
        scratch_shapes=[pltpu.VMEM((1, LANES), F32), pltpu.VMEM((H, 1), F32)],
        compiler_params=_params(("parallel", "arbitrary")),
        name="fox_proj",
    )(x, nw, w_all, wf, wft, bcol, brow, qk_norm)

    t = min(ATTN_TILE, s)
    nq = s // t
    G = ATTN_HEADS_PER_STEP
    qblk = pl.BlockSpec((t, G * DH), lambda bi, hi, qi: (bi * nq + qi, hi))
    kblk = pl.BlockSpec((s, G * DH), lambda bi, hi, qi: (bi, hi))
    att = pl.pallas_call(
        _fox_attn_kernel,
        grid=(b, H // G, nq),
        in_specs=[qblk, kblk, kblk, qblk,
                  pl.BlockSpec((t, LANES), lambda bi, hi, qi: (bi * nq + qi, 0)),
                  pl.BlockSpec((G, 1, s), lambda bi, hi, qi: (hi, 0, bi))],
        out_specs=qblk,
        out_shape=jax.ShapeDtypeStruct((m, D), BF16),
        scratch_shapes=[pltpu.VMEM((G, nq, t, t), F32), pltpu.VMEM((G, t, LANES), F32),
                        pltpu.VMEM((G, t, LANES), F32), pltpu.VMEM((G, t, DH), F32),
                        pltpu.VMEM((G, t, LANES), F32)],
        compiler_params=_params(("parallel", "parallel", "arbitrary")),
        name="fox_attn",
    )(q, k, v, so, fcol, frow)
    return att


def _gla_level_matrix(L, levels):
    t = np.arange(L)[:, None]
    c = np.arange(L)[None, :]
    blocks = [(c <= t)]
    for j in range(levels):
        hs = 1 << j
        mid = (t & ~(2 * hs - 1)) + hs
        upper = (t & hs) != 0
        blocks.append(np.where(upper, (c > mid) & (c <= t), (c > t) & (c <= mid)))
    return np.concatenate(blocks, axis=0).astype(np.float32)


def _gla_proj_kernel(x_ref, nw_ref, w_ref, wg_ref, wup_ref, bg_ref,
                     q_ref, k_ref, v_ref, sr_ref, la_ref):
    xn = _rms(x_ref[...], nw_ref[...]).astype(BF16)
    DKT = GLA_DK_TOTAL
    q_ref[...] = (_dot(xn, w_ref[:, :DKT]) * GLA_DK ** -0.5).astype(BF16)
    k_ref[...] = _dot(xn, w_ref[:, DKT:2 * DKT]).astype(BF16)
    v_ref[...] = _dot(xn, w_ref[:, 2 * DKT:2 * DKT + D_MODEL]).astype(BF16)
    r = _dot(xn, w_ref[:, 2 * DKT + D_MODEL:2 * DKT + 2 * D_MODEL])
    sr_ref[...] = (r * _sigmoid(r)).astype(BF16)
    glr = _dot(xn, wg_ref[...])
    z = _dot(glr.astype(BF16), wup_ref[...]) + bg_ref[...]
    la_ref[...] = _log_sigmoid(z) * (1.0 / GLA_TAU)


def _gla_core_kernel(q_ref, k_ref, v_ref, sr_ref, la_ref, lvl_ref, nw_ref, o_ref, st_ref):
    L = q_ref.shape[0]
    H, DK, DV = GLA_HEADS, GLA_DK, GLA_DV
    levels = lvl_ref.shape[0] // L - 1

    @pl.when(pl.program_id(1) == 0)
    def _():
        st_ref[...] = jnp.zeros_like(st_ref)

    hi, mid, _ = _split3(la_ref[...])
    args = _dot(lvl_ref[...], jnp.concatenate([hi, mid], axis=0))
    row = lax.broadcasted_iota(jnp.int32, (L, L), 0)
    col = lax.broadcasted_iota(jnp.int32, (L, L), 1)
    diff = jnp.where(row > col, row ^ col, 0)

    for h in range(H):
        ks = slice(h * DK, (h + 1) * DK)
        vs = slice(h * DV, (h + 1) * DV)
        q = q_ref[:, ks].astype(F32)
        k = k_ref[:, ks].astype(F32)
        v = v_ref[:, vs]
        g = args[0:L, ks]
        a = jnp.where(row == col, jnp.sum(q * k, axis=-1, keepdims=True), 0.0)
        for j in range(levels):
            e = jnp.exp(args[(j + 1) * L:(j + 2) * L, ks])
            aj = _dot_nt((q * e).astype(BF16), (k * e).astype(BF16))
            a = jnp.where((diff >> j) == 1, aj, a)
        st = st_ref[h]
        o = _dot(a.astype(BF16), v) + _dot_nt((q * jnp.exp(g)).astype(BF16), st.astype(BF16))
        g_last = g[L - 1:L, :]
        kd = (k * jnp.exp(g_last - g)).astype(BF16)
        st_ref[h] = st * jnp.exp(g_last) + _dot_tn(v, kd)
        on = _rms(o, nw_ref[:, vs])
        o_ref[:, vs] = (on * sr_ref[:, vs].astype(F32)).astype(BF16)


def _gla_mixer(x, b, s, nw, w_all, jm, w_in, w_gate_up, b_gate, norm_w):
    m = b * s
    D, H, DKT, R = D_MODEL, GLA_HEADS, GLA_DK_TOTAL, GLA_RANK
    tm = min(GLA_ROW_TILE, m)
    wg = jnp.pad(w_in[:, 2 * DKT + 2 * D:].astype(BF16), ((0, 0), (0, LANES - R)))
    wup = jnp.pad(w_gate_up.astype(BF16), ((0, LANES - R), (0, 0)))

    row = pl.BlockSpec((tm, D), lambda i: (i, 0))
    half = pl.BlockSpec((tm, DKT), lambda i: (i, 0))
    q, k, v, sr, la = pl.pallas_call(
        _gla_proj_kernel,
        grid=(m // tm,),
        in_specs=[row, _resident((1, D)), _pick((D, 2 * DKT + 2 * D + R), jm), _resident((D, LANES)),
                  _resident((LANES, DKT)), _resident((1, DKT))],
        out_specs=[half, half, row, row, half],
        out_shape=[jax.ShapeDtypeStruct((m, DKT), BF16), jax.ShapeDtypeStruct((m, DKT), BF16),
                   jax.ShapeDtypeStruct((m, D), BF16), jax.ShapeDtypeStruct((m, D), BF16),
                   jax.ShapeDtypeStruct((m, DKT), F32)],
        compiler_params=_params(("parallel",)),
        name="gla_proj",
    )(x, nw, w_all, wg, wup, b_gate.reshape(1, DKT))

    L = min(GLA_CHUNK, s)
    levels = L.bit_length() - 1
    nc = s // L
    lvl = _gla_level_matrix(L, levels)
    lvl = jnp.asarray(np.concatenate([lvl, lvl], axis=1), dtype=BF16)
    kblk = pl.BlockSpec((L, DKT), lambda i, j: (i * nc + j, 0))
    vblk = pl.BlockSpec((L, D), lambda i, j: (i * nc + j, 0))
    og = pl.pallas_call(
        _gla_core_kernel,
        grid=(b, nc),
        in_specs=[kblk, kblk, vblk, vblk, kblk, _resident(((levels + 1) * L, 2 * L)), _resident((1, D))],
        out_specs=vblk,
        out_shape=jax.ShapeDtypeStruct((m, D), BF16),
        scratch_shapes=[pltpu.VMEM((H, GLA_DV, GLA_DK), F32)],
        compiler_params=_params(("parallel", "arbitrary")),
        name="gla_core",
    )(q, k, v, sr, la, lvl, norm_w.reshape(1, D))
    return og


def kernel(x, norm_w, ffn_w_in, ffn_w_out, mlstm_w_in, mlstm_b_gate, mlstm_conv_w, mlstm_norm_w, mlstm_w_out,
           fox_w_in, fox_b_f, fox_qk_norm, fox_w_out, gla_w_in, gla_w_gate_up, gla_b_gate, gla_norm_w,
           gla_w_out, final_norm_w):
    b, s, d = x.shape
    depth = norm_w.shape[0]
    xf = x.reshape(b * s, d)
    fw = final_norm_w.reshape(1, d)
    ffn_in, ffn_out = ffn_w_in.astype(BF16), ffn_w_out.astype(BF16)
    mlstm_in, mlstm_out = mlstm_w_in.astype(BF16), mlstm_w_out.astype(BF16)
    fox_in, fox_out = fox_w_in.astype(BF16), fox_w_out.astype(BF16)
    gla_in, gla_out = gla_w_in.astype(BF16), gla_w_out.astype(BF16)
    for layer in range(depth):
        kind, j = layer % N_MIXERS, layer // N_MIXERS
        xf = _ffn(xf, norm_w[layer, 0].reshape(1, d), ffn_in, ffn_out, (layer, 0), fw, False)
        nw = norm_w[layer, 1].reshape(1, d)
        if kind == 0:
            a = _mlstm_mixer(xf, b, s, nw, mlstm_in, j, mlstm_w_in[j], mlstm_b_gate[j], mlstm_conv_w[j],
                             mlstm_norm_w[j])
            mix = (a, mlstm_out, j)
        elif kind == 1:
            a = _fox_mixer(xf, b, s, nw, fox_in, j, fox_w_in[j], fox_b_f[j], fox_qk_norm[j])
            mix = (a, fox_out, j)
        else:
            a = _gla_mixer(xf, b, s, nw, gla_in, j, gla_w_in[j], gla_w_gate_up[j], gla_b_gate[j], gla_norm_w[j])
            mix = (a, gla_out, j)
        xf = _ffn(xf, norm_w[layer, 2].reshape(1, d), ffn_in, ffn_out, (layer, 1), fw, layer == depth - 1, mix)
    return xf.reshape(b, s, d)
```

```python
import functools

import numpy as np
import jax
import jax.numpy as jnp
from jax import lax
from jax.experimental import pallas as pl
from jax.experimental.pallas import tpu as pltpu

F32 = jnp.float32
BF16 = jnp.bfloat16

EPS = 1e-6
D_MODEL = 1024
D_FF = 2816
MLSTM_HEADS = 4
MLSTM_DH = D_MODEL // MLSTM_HEADS
CONV_WIDTH = 4
FOX_HEADS = 8
FOX_DH = D_MODEL // FOX_HEADS
GLA_HEADS = 4
GLA_DK_TOTAL = D_MODEL // 2
GLA_DK = GLA_DK_TOTAL // GLA_HEADS
GLA_DV = D_MODEL // GLA_HEADS
GLA_RANK = 16
GLA_TAU = 16.0
N_MIXERS = 3

LANES = 128
SUBLANES = 8
VMEM_LIMIT_BYTES = 56 * 1024 * 1024

ROW_TILE = 512
FFN_ROW_TILE = 1024
MLSTM_CHUNK = 256
MLSTM_CHUNKS_PER_STEP = 4
MLSTM_CONV_COLS = 512
MLSTM_ROW_TILE = 256
GLA_CHUNK = 128
GLA_ROW_TILE = 1024
GLA_CHUNKS_PER_STEP = 4
ATTN_TILE = 512
ATTN_HEADS_PER_STEP = 4
LOG2E = 1.4426950408889634


def _dot(a, b):
    return jnp.dot(a, b, preferred_element_type=F32)


def _dot_nt(a, b):
    return lax.dot_general(a, b, (((1,), (1,)), ((), ())), preferred_element_type=F32)


def _dot_tn(a, b):
    return lax.dot_general(a, b, (((0,), (0,)), ((), ())), preferred_element_type=F32)


def _rms(x, w):
    ms = jnp.mean(x * x, axis=-1, keepdims=True)
    return x * lax.rsqrt(ms + EPS) * w


def _sigmoid(x):
    return 1.0 / (1.0 + jnp.exp(-x))


def _log_sigmoid(x):
    return jnp.minimum(x, 0.0) - jnp.log(1.0 + jnp.exp(-jnp.abs(x)))


def _split3(x):
    hi = x.astype(BF16)
    r1 = x - hi.astype(F32)
    mid = r1.astype(BF16)
    lo = (r1 - mid.astype(F32)).astype(BF16)
    return hi, mid, lo


def _cumsum_rows(tril, x):
    hi, mid, lo = _split3(x)
    return _dot(tril, hi) + _dot(tril, mid) + _dot(tril, lo)


def _cumsum_lanes(x, triu):
    hi, mid, lo = _split3(x)
    return _dot(hi, triu) + _dot(mid, triu) + _dot(lo, triu)


def _tri(n):
    row = lax.broadcasted_iota(jnp.int32, (n, n), 0)
    col = lax.broadcasted_iota(jnp.int32, (n, n), 1)
    tril = jnp.where(row >= col, 1.0, 0.0).astype(BF16)
    triu = jnp.where(row <= col, 1.0, 0.0).astype(BF16)
    return row, col, tril, triu


def _resident(shape):
    zeros = (0,) * len(shape)
    return pl.BlockSpec(shape, lambda *_: zeros, pipeline_mode=pl.Buffered(1))


def _pick(shape, *lead):
    zeros = (0,) * len(shape)
    return pl.BlockSpec((None,) * len(lead) + tuple(shape), lambda *_: tuple(lead) + zeros,
                        pipeline_mode=pl.Buffered(1))


def _params(semantics):
    return pltpu.CompilerParams(dimension_semantics=semantics, vmem_limit_bytes=VMEM_LIMIT_BYTES)


def _row_tile(s):
    return min(ROW_TILE, s)


def _ffn_kernel(*refs, mixed, final):
    if mixed:
        x_ref, a_ref, wmix_ref, nw_ref, win_ref, wout_ref, fw_ref, o_ref = refs
        x = x_ref[...] + _dot(a_ref[...], wmix_ref[...])
    else:
        x_ref, nw_ref, win_ref, wout_ref, fw_ref, o_ref = refs
        x = x_ref[...]
    xn = _rms(x, nw_ref[...]).astype(BF16)
    g = _dot(xn, win_ref[:, :D_FF])
    u = _dot(xn, win_ref[:, D_FF:])
    a = (g * _sigmoid(g) * u).astype(BF16)
    y = x + 0.5 * _dot(a, wout_ref[...])
    if final:
        y = _rms(y, fw_ref[...])
    o_ref[...] = y


def _ffn(x, nw, w_in, w_out, sel, fw, final, mix=None):
    m = x.shape[0]
    tm = min(FFN_ROW_TILE, m)
    row = pl.BlockSpec((tm, D_MODEL), lambda i: (i, 0))
    weights = [_resident((1, D_MODEL)), _pick((D_MODEL, 2 * D_FF), *sel), _pick((D_FF, D_MODEL), *sel),
               _resident((1, D_MODEL))]
    if mix is None:
        in_specs, args = [row] + weights, (x, nw, w_in, w_out, fw)
    else:
        a, w_mix, j = mix
        in_specs = [row, row, _pick((D_MODEL, D_MODEL), j)] + weights
        args = (x, a, w_mix, nw, w_in, w_out, fw)
    return pl.pallas_call(
        functools.partial(_ffn_kernel, mixed=mix is not None, final=final),
        grid=(m // tm,),
        in_specs=in_specs,
        out_specs=row,
        out_shape=jax.ShapeDtypeStruct((m, D_MODEL), F32),
        compiler_params=_params(("parallel",)),
        name="ffn" + ("_mix" if mix is not None else "") + ("_final" if final else ""),
    )(*args)


def _mlstm_proj_kernel(x_ref, nw_ref, w_ref, wg_ref, wgt_ref, bcol_ref, brow_ref, cw_ref,
                       q_ref, k_ref, v_ref, so_ref, gcol_ref, grow_ref, carry_ref):
    tm = x_ref.shape[0]

    @pl.when(pl.program_id(1) == 0)
    def _():
        carry_ref[:SUBLANES, :] = jnp.zeros((SUBLANES, carry_ref.shape[1]), F32)

    xn = _rms(x_ref[...], nw_ref[...]).astype(BF16)
    for c in range(2 * D_MODEL // MLSTM_CONV_COLS):
        cs = slice(c * MLSTM_CONV_COLS, (c + 1) * MLSTM_CONV_COLS)
        carry_ref[SUBLANES:, cs] = _dot(xn, w_ref[:, cs])
        cw = cw_ref[:, cs]
        conv = carry_ref[SUBLANES:, cs] * cw[CONV_WIDTH - 1:CONV_WIDTH]
        for j in range(1, CONV_WIDTH):
            conv = conv + carry_ref[SUBLANES - j:SUBLANES - j + tm, cs] * cw[CONV_WIDTH - 1 - j:CONV_WIDTH - j]
        carry_ref[:SUBLANES, cs] = carry_ref[tm:, cs]
        act = conv * _sigmoid(conv)
        if c * MLSTM_CONV_COLS < D_MODEL:
            q_ref[:, cs] = (act * MLSTM_DH ** -0.5).astype(BF16)
        else:
            k_ref[:, slice(cs.start - D_MODEL, cs.stop - D_MODEL)] = act.astype(BF16)
    v_ref[...] = _dot(xn, w_ref[:, 2 * D_MODEL:3 * D_MODEL]).astype(BF16)
    so_ref[...] = _sigmoid(_dot(xn, w_ref[:, 3 * D_MODEL:4 * D_MODEL])).astype(BF16)
    gcol_ref[...] = _dot(xn, wg_ref[...]) + bcol_ref[...]
    grow_ref[...] = _dot_nt(wgt_ref[...], xn) + brow_ref[...]


def _select_lane(parts, sel):
    return _dot(parts[0], sel) + _dot(parts[1], sel) + _dot(parts[2], sel)


def _mlstm_core_kernel(q_ref, k_ref, v_ref, so_ref, gcol_ref, grow_ref, nw_ref, o_ref, cn_ref, m_ref):
    L = MLSTM_CHUNK if q_ref.shape[0] % MLSTM_CHUNK == 0 else q_ref.shape[0]

    @pl.when(pl.program_id(1) == 0)
    def _():
        cn_ref[...] = jnp.zeros_like(cn_ref)
        m_ref[...] = jnp.zeros_like(m_ref)

    _, _, tril, triu = _tri(L)
    row = lax.broadcasted_iota(jnp.int32, (L, LANES), 0)
    lane = lax.broadcasted_iota(jnp.int32, (L, LANES), 1)
    sel_row = lax.broadcasted_iota(jnp.int32, (LANES, LANES), 0)
    ones = jnp.ones((L, LANES), BF16)
    for c in range(q_ref.shape[0] // L):
        _mlstm_chunk(slice(c * L, (c + 1) * L), q_ref, k_ref, v_ref, so_ref, gcol_ref, grow_ref, nw_ref, o_ref,
                     cn_ref, m_ref, tril, triu, row, lane, sel_row, ones)


def _mlstm_chunk(rs, q_ref, k_ref, v_ref, so_ref, gcol_ref, grow_ref, nw_ref, o_ref, cn_ref, m_ref,
                 tril, triu, row, lane, sel_row, ones):
    L = rs.stop - rs.start
    H, DH = MLSTM_HEADS, MLSTM_DH
    nl = L // LANES
    gcol = gcol_ref[rs, :]
    grow = grow_ref[:, rs]
    bc_all = _cumsum_rows(tril, _log_sigmoid(gcol))
    br_all = _cumsum_lanes(_log_sigmoid(grow), triu)
    gcol_parts = _split3(gcol)
    bc_parts = _split3(bc_all)

    for h in range(H):
        hs = slice(h * DH, (h + 1) * DH)
        q = q_ref[rs, hs]
        k = k_ref[rs, hs]
        v_aug = jnp.concatenate([v_ref[rs, hs], ones], axis=1)
        icb = _select_lane(gcol_parts, jnp.where(sel_row == h, 1.0, 0.0).astype(BF16))
        bcb = _select_lane(bc_parts, jnp.where(sel_row == H + h, 1.0, 0.0).astype(BF16))
        a_row = grow[h:h + 1, :] - br_all[H + h:H + h + 1, :]
        m_prev = m_ref[h]
        cn_prev = cn_ref[h]

        a_tiles = [jnp.where(row >= lane + j * LANES, a_row[:, j * LANES:(j + 1) * LANES], -jnp.inf)
                   for j in range(nl)]
        a_max = a_tiles[0]
        for j in range(1, nl):
            a_max = jnp.maximum(a_max, a_tiles[j])
        u = jnp.maximum(m_prev, jnp.max(a_max, axis=-1, keepdims=True))
        inter = jnp.exp(m_prev - u)
        s = _dot_nt(q, k)
        sc = jnp.concatenate([s[:, j * LANES:(j + 1) * LANES] * jnp.exp(a_tiles[j] - u) for j in range(nl)],
                             axis=1)
        tot = _dot(sc.astype(BF16), v_aug) + jnp.concatenate([inter] * (DH // LANES + 1), axis=1) * _dot(
            q, cn_prev.astype(BF16))
        den = tot[:, DH:]
        r = 1.0 / jnp.maximum(jnp.abs(den), jnp.exp(-(bcb + u)))
        h_out = tot[:, :DH] * jnp.concatenate([r] * (DH // LANES), axis=1)

        b_last = bcb[L - 1:L, :]
        m_new = b_last + jnp.maximum(m_prev, jnp.max(a_row, axis=-1, keepdims=True))
        decay = jnp.exp(b_last + m_prev - m_new)
        w = jnp.exp(b_last - bcb + icb - m_new)
        kw = (k.astype(F32) * jnp.concatenate([w] * (DH // LANES), axis=1)).astype(BF16)
        cn_ref[h] = jnp.concatenate([decay] * (DH // LANES + 1), axis=1) * cn_prev + _dot_tn(kw, v_aug)
        m_ref[h] = m_new

        hn = _rms(h_out, nw_ref[:, hs])
        o_ref[rs, hs] = (hn * so_ref[rs, hs].astype(F32)).astype(BF16)


def _mlstm_mixer(x, b, s, nw, w_all, jm, w_in, b_gate, conv_w, norm_w):
    m = b * s
    D, H = D_MODEL, MLSTM_HEADS
    tm = min(MLSTM_ROW_TILE, s)
    nt = s // tm
    wgate = w_in[:, 4 * D:].astype(BF16)
    wg = jnp.pad(wgate, ((0, 0), (0, LANES - 2 * H)))
    wgt = wgate.T
    bcol = jnp.pad(b_gate, (0, LANES - 2 * H)).reshape(1, LANES)
    brow = b_gate.reshape(2 * H, 1)

    row = pl.BlockSpec((tm, D), lambda i, j: (i * nt + j, 0))
    q, k, v, so, gcol, grow = pl.pallas_call(
        _mlstm_proj_kernel,
        grid=(b, nt),
        in_specs=[row, _resident((1, D)), _pick((D, 4 * D + 2 * H), jm),
                  _resident((D, LANES)), _resident((2 * H, D)), _resident((1, LANES)), _resident((2 * H, 1)),
                  _resident((CONV_WIDTH, 2 * D))],
        out_specs=[row, row, row, row,
                   pl.BlockSpec((tm, LANES), lambda i, j: (i * nt + j, 0)),
                   pl.BlockSpec((2 * H, tm), lambda i, j: (0, i * nt + j))],
        out_shape=[jax.ShapeDtypeStruct((m, D), BF16)] * 4 + [
            jax.ShapeDtypeStruct((m, LANES), F32), jax.ShapeDtypeStruct((2 * H, m), F32)],
        scratch_shapes=[pltpu.VMEM((SUBLANES + tm, 2 * D), F32)],
        compiler_params=_params(("parallel", "arbitrary")),
        name="mlstm_proj",
    )(x, nw, w_all, wg, wgt, bcol, brow, conv_w)

    L = min(MLSTM_CHUNKS_PER_STEP * MLSTM_CHUNK, s)
    nc = s // L
    blk = pl.BlockSpec((L, D), lambda i, j: (i * nc + j, 0))
    hg = pl.pallas_call(
        _mlstm_core_kernel,
        grid=(b, nc),
        in_specs=[blk, blk, blk, blk,
                  pl.BlockSpec((L, LANES), lambda i, j: (i * nc + j, 0)),
                  pl.BlockSpec((2 * H, L), lambda i, j: (0, i * nc + j)),
                  _resident((1, D))],
        out_specs=blk,
        out_shape=jax.ShapeDtypeStruct((m, D), BF16),
        scratch_shapes=[pltpu.VMEM((H, MLSTM_DH, MLSTM_DH + LANES), F32), pltpu.VMEM((H, 1, LANES), F32)],
        compiler_params=_params(("parallel", "arbitrary")),
        name="mlstm_core",
    )(q, k, v, so, gcol, grow, norm_w.reshape(1, D))
    return hg


def _fox_proj_kernel(x_ref, nw_ref, w_ref, wf_ref, wft_ref, bcol_ref, brow_ref, qkn_ref,
                     q_ref, k_ref, v_ref, so_ref, fcol_ref, frow_ref, ccol_ref, crow_ref):
    tm = x_ref.shape[0]

    @pl.when(pl.program_id(1) == 0)
    def _():
        ccol_ref[...] = jnp.zeros_like(ccol_ref)
        crow_ref[...] = jnp.zeros_like(crow_ref)

    xn = _rms(x_ref[...], nw_ref[...]).astype(BF16)
    q = _dot(xn, w_ref[:, :D_MODEL])
    k = _dot(xn, w_ref[:, D_MODEL:2 * D_MODEL])
    qkn = qkn_ref[...]
    for h in range(FOX_HEADS):
        hs = slice(h * FOX_DH, (h + 1) * FOX_DH)
        q_ref[:, hs] = (_rms(q[:, hs], qkn[0:1]) * (FOX_DH ** -0.5 * LOG2E)).astype(BF16)
        k_ref[:, hs] = _rms(k[:, hs], qkn[1:2]).astype(BF16)
    v_ref[...] = _dot(xn, w_ref[:, 2 * D_MODEL:3 * D_MODEL]).astype(BF16)
    so_ref[...] = _sigmoid(_dot(xn, w_ref[:, 3 * D_MODEL:4 * D_MODEL])).astype(BF16)

    _, _, tril, triu = _tri(tm)
    lf_col = _log_sigmoid(_dot(xn, wf_ref[...]) + bcol_ref[...])
    lf_row = _log_sigmoid(_dot_nt(wft_ref[...], xn) + brow_ref[...])
    f_col = _cumsum_rows(tril, lf_col) + ccol_ref[...]
    f_row = _cumsum_lanes(lf_row, triu) + crow_ref[...]
    fcol_ref[...] = f_col * LOG2E
    for h in range(FOX_HEADS):
        frow_ref[h] = f_row[h:h + 1, :] * LOG2E
    ccol_ref[...] = f_col[tm - 1:tm, :]
    crow_ref[...] = f_row[:, tm - 1:tm]


def _fox_attn_kernel(q_ref, k_ref, v_ref, so_ref, fcol_ref, frow_ref, o_ref,
                     s_ref, m_ref, l_ref, acc_ref, fq_ref):
    t = q_ref.shape[0]
    G = s_ref.shape[0]
    nl = t // LANES
    hp = pl.program_id(1)
    qi = pl.program_id(2)

    lane = lax.broadcasted_iota(jnp.int32, (t, LANES), 1)
    row = lax.broadcasted_iota(jnp.int32, (t, LANES), 0)
    for g in range(G):
        fq = jnp.sum(jnp.where(lane == hp * G + g, fcol_ref[...], 0.0), axis=-1, keepdims=True)
        fq_ref[g] = jnp.broadcast_to(fq, (t, LANES))
    m_ref[...] = jnp.full_like(m_ref, -jnp.inf)
    l_ref[...] = jnp.zeros_like(l_ref)
    acc_ref[...] = jnp.zeros_like(acc_ref)

    def scores(kt, diagonal):
        off = pl.multiple_of(kt * t, t)
        for g in range(G):
            hs = slice(g * FOX_DH, (g + 1) * FOX_DH)
            s = _dot_nt(q_ref[:, hs], k_ref[pl.ds(off, t), hs])
            fk = frow_ref[g, :, pl.ds(off, t)]
            fqb = fq_ref[g]
            m = m_ref[g]
            for j in range(nl):
                ls = slice(j * LANES, (j + 1) * LANES)
                sj = s[:, ls] + (fqb - fk[:, ls])
                if diagonal:
                    sj = jnp.where(row >= lane + j * LANES, sj, -jnp.inf)
                s_ref[g, kt, :, ls] = sj
                m = jnp.maximum(m, sj)
            m_ref[g] = m

    def scores_body(kt, carry):
        scores(kt, False)
        return carry

    lax.fori_loop(0, qi, scores_body, 0)
    scores(qi, True)
    for g in range(G):
        m_ref[g] = jnp.broadcast_to(jnp.max(m_ref[g], axis=-1, keepdims=True), (t, LANES))

    def accumulate(kt, carry):
        off = pl.multiple_of(kt * t, t)
        for g in range(G):
            hs = slice(g * FOX_DH, (g + 1) * FOX_DH)
            mb = m_ref[g]
            l = l_ref[g]
            ps = []
            for j in range(nl):
                p = jnp.exp2(s_ref[g, kt, :, j * LANES:(j + 1) * LANES] - mb)
                l = l + p
                ps.append(p.astype(BF16))
            l_ref[g] = l
            acc_ref[g] += _dot(jnp.concatenate(ps, axis=1), v_ref[pl.ds(off, t), hs])
        return carry

    lax.fori_loop(0, qi + 1, accumulate, 0)
    for g in range(G):
        hs = slice(g * FOX_DH, (g + 1) * FOX_DH)
        l = jnp.sum(l_ref[g], axis=-1, keepdims=True)
        o_ref[:, hs] = (acc_ref[g] * (1.0 / l) * so_ref[:, hs].astype(F32)).astype(BF16)


def _fox_mixer(x, b, s, nw, w_all, jm, w_in, b_f, qk_norm):
    m = b * s
    D, H, DH = D_MODEL, FOX_HEADS, FOX_DH
    tm = _row_tile(s)
    nt = s // tm
    wgate = w_in[:, 4 * D:].astype(BF16)
    wf = jnp.pad(wgate, ((0, 0), (0, LANES - H)))
    wft = wgate.T
    bcol = jnp.pad(b_f, (0, LANES - H)).reshape(1, LANES)
    brow = b_f.reshape(H, 1)

    row = pl.BlockSpec((tm, D), lambda i, j: (i * nt + j, 0))
    q, k, v, so, fcol, frow = pl.pallas_call(
        _fox_proj_kernel,
        grid=(b, nt),
        in_specs=[row, _resident((1, D)), _pick((D, 4 * D + H), jm), _resident((D, LANES)), _resident((H, D)),
                  _resident((1, LANES)), _resident((H, 1)), _resident((2, DH))],
        out_specs=[row, row, row, row,
                   pl.BlockSpec((tm, LANES), lambda i, j: (i * nt + j, 0)),
                   pl.BlockSpec((H, 1, tm), lambda i, j: (0, 0, i * nt + j))],
        out_shape=[jax.ShapeDtypeStruct((m, D), BF16)] * 4 + [
            jax.ShapeDtypeStruct((m, LANES), F32), jax.ShapeDtypeStruct((H, 1, m), F32)],
        scratch_shapes=[pltpu.VMEM((1, LANES), F32), pltpu.VMEM((H, 1), F32)],
        compiler_params=_params(("parallel", "arbitrary")),
        name="fox_proj",
    )(x, nw, w_all, wf, wft, bcol, brow, qk_norm)

    t = min(ATTN_TILE, s)
    nq = s // t
    G = ATTN_HEADS_PER_STEP
    qblk = pl.BlockSpec((t, G * DH), lambda bi, hi, qi: (bi * nq + qi, hi))
    kblk = pl.BlockSpec((s, G * DH), lambda bi, hi, qi: (bi, hi))
    att = pl.pallas_call(
        _fox_attn_kernel,
        grid=(b, H // G, nq),
        in_specs=[qblk, kblk, kblk, qblk,
                  pl.BlockSpec((t, LANES), lambda bi, hi, qi: (bi * nq + qi, 0)),
                  pl.BlockSpec((G, 1, s), lambda bi, hi, qi: (hi, 0, bi))],
        out_specs=qblk,
        out_shape=jax.ShapeDtypeStruct((m, D), BF16),
        scratch_shapes=[pltpu.VMEM((G, nq, t, t), F32), pltpu.VMEM((G, t, LANES), F32),
                        pltpu.VMEM((G, t, LANES), F32), pltpu.VMEM((G, t, DH), F32),
                        pltpu.VMEM((G, t, LANES), F32)],
        compiler_params=_params(("parallel", "parallel", "arbitrary")),
        name="fox_attn",
    )(q, k, v, so, fcol, frow)
    return att


def _gla_level_matrix(L, levels):
    t = np.arange(L)[:, None]
    c = np.arange(L)[None, :]
    blocks = [(c <= t)]
    for j in range(levels):
        hs = 1 << j
        mid = (t & ~(2 * hs - 1)) + hs
        upper = (t & hs) != 0
        blocks.append(np.where(upper, (c > mid) & (c <= t), (c > t) & (c <= mid)))
    return np.concatenate(blocks, axis=0).astype(np.float32)


def _gla_proj_kernel(x_ref, nw_ref, w_ref, wg_ref, wup_ref, bg_ref,
                     q_ref, k_ref, v_ref, sr_ref, la_ref):
    xn = _rms(x_ref[...], nw_ref[...]).astype(BF16)
    DKT = GLA_DK_TOTAL
    q_ref[...] = (_dot(xn, w_ref[:, :DKT]) * GLA_DK ** -0.5).astype(BF16)
    k_ref[...] = _dot(xn, w_ref[:, DKT:2 * DKT]).astype(BF16)
    v_ref[...] = _dot(xn, w_ref[:, 2 * DKT:2 * DKT + D_MODEL]).astype(BF16)
    r = _dot(xn, w_ref[:, 2 * DKT + D_MODEL:2 * DKT + 2 * D_MODEL])
    sr_ref[...] = (r * _sigmoid(r)).astype(BF16)
    glr = _dot(xn, wg_ref[...])
    z = _dot(glr.astype(BF16), wup_ref[...]) + bg_ref[...]
    la_ref[...] = _log_sigmoid(z) * (1.0 / GLA_TAU)


def _gla_core_kernel(q_ref, k_ref, v_ref, sr_ref, la_ref, lvl_ref, nw_ref, o_ref, st_ref):
    L = GLA_CHUNK if q_ref.shape[0] % GLA_CHUNK == 0 else q_ref.shape[0]
    H, DK, DV = GLA_HEADS, GLA_DK, GLA_DV
    levels = lvl_ref.shape[0] // L - 1

    @pl.when(pl.program_id(1) == 0)
    def _():
        st_ref[...] = jnp.zeros_like(st_ref)

    row = lax.broadcasted_iota(jnp.int32, (L, L), 0)
    col = lax.broadcasted_iota(jnp.int32, (L, L), 1)
    diff = jnp.where(row > col, row ^ col, 0)

    for c in range(q_ref.shape[0] // L):
        rs = slice(c * L, (c + 1) * L)
        hi, mid, _ = _split3(la_ref[rs, :])
        args = _dot(lvl_ref[...], jnp.concatenate([hi, mid], axis=0))
        for h in range(H):
            ks = slice(h * DK, (h + 1) * DK)
            vs = slice(h * DV, (h + 1) * DV)
            q = q_ref[rs, ks].astype(F32)
            k = k_ref[rs, ks].astype(F32)
            v = v_ref[rs, vs]
            g = args[0:L, ks]
            a = jnp.where(row == col, jnp.sum(q * k, axis=-1, keepdims=True), 0.0)
            for j in range(levels):
                e = jnp.exp(args[(j + 1) * L:(j + 2) * L, ks])
                aj = _dot_nt((q * e).astype(BF16), (k * e).astype(BF16))
                a = jnp.where((diff >> j) == 1, aj, a)
            st = st_ref[h]
            o = _dot(a.astype(BF16), v) + _dot_nt((q * jnp.exp(g)).astype(BF16), st.astype(BF16))
            g_last = g[L - 1:L, :]
            kd = (k * jnp.exp(g_last - g)).astype(BF16)
            st_ref[h] = st * jnp.exp(g_last) + _dot_tn(v, kd)
            on = _rms(o, nw_ref[:, vs])
            o_ref[rs, vs] = (on * sr_ref[rs, vs].astype(F32)).astype(BF16)


def _gla_mixer(x, b, s, nw, w_all, jm, w_in, w_gate_up, b_gate, norm_w):
    m = b * s
    D, H, DKT, R = D_MODEL, GLA_HEADS, GLA_DK_TOTAL, GLA_RANK
    tm = min(GLA_ROW_TILE, m)
    wg = jnp.pad(w_in[:, 2 * DKT + 2 * D:].astype(BF16), ((0, 0), (0, LANES - R)))
    wup = jnp.pad(w_gate_up.astype(BF16), ((0, LANES - R), (0, 0)))

    row = pl.BlockSpec((tm, D), lambda i: (i, 0))
    half = pl.BlockSpec((tm, DKT), lambda i: (i, 0))
    q, k, v, sr, la = pl.pallas_call(
        _gla_proj_kernel,
        grid=(m // tm,),
        in_specs=[row, _resident((1, D)), _pick((D, 2 * DKT + 2 * D + R), jm), _resident((D, LANES)),
                  _resident((LANES, DKT)), _resident((1, DKT))],
        out_specs=[half, half, row, row, half],
        out_shape=[jax.ShapeDtypeStruct((m, DKT), BF16), jax.ShapeDtypeStruct((m, DKT), BF16),
                   jax.ShapeDtypeStruct((m, D), BF16), jax.ShapeDtypeStruct((m, D), BF16),
                   jax.ShapeDtypeStruct((m, DKT), F32)],
        compiler_params=_params(("parallel",)),
        name="gla_proj",
    )(x, nw, w_all, wg, wup, b_gate.reshape(1, DKT))

    L = min(GLA_CHUNK, s)
    levels = L.bit_length() - 1
    rows = min(GLA_CHUNKS_PER_STEP * L, s)
    nc = s // rows
    lvl = _gla_level_matrix(L, levels)
    lvl = jnp.asarray(np.concatenate([lvl, lvl], axis=1), dtype=BF16)
    kblk = pl.BlockSpec((rows, DKT), lambda i, j: (i * nc + j, 0))
    vblk = pl.BlockSpec((rows, D), lambda i, j: (i * nc + j, 0))
    og = pl.pallas_call(
        _gla_core_kernel,
        grid=(b, nc),
        in_specs=[kblk, kblk, vblk, vblk, kblk, _resident(((levels + 1) * L, 2 * L)), _resident((1, D))],
        out_specs=vblk,
        out_shape=jax.ShapeDtypeStruct((m, D), BF16),
        scratch_shapes=[pltpu.VMEM((H, GLA_DV, GLA_DK), F32)],
        compiler_params=_params(("parallel", "arbitrary")),
        name="gla_core",
    )(q, k, v, sr, la, lvl, norm_w.reshape(1, D))
    return og


def kernel(x, norm_w, ffn_w_in, ffn_w_out, mlstm_w_in, mlstm_b_gate, mlstm_conv_w, mlstm_norm_w, mlstm_w_out,
           fox_w_in, fox_b_f, fox_qk_norm, fox_w_out, gla_w_in, gla_w_gate_up, gla_b_gate, gla_norm_w,
           gla_w_out, final_norm_w):
    b, s, d = x.shape
    depth = norm_w.shape[0]
    xf = x.reshape(b * s, d)
    fw = final_norm_w.reshape(1, d)
    ffn_in, ffn_out = ffn_w_in.astype(BF16), ffn_w_out.astype(BF16)
    mlstm_in, mlstm_out = mlstm_w_in.astype(BF16), mlstm_w_out.astype(BF16)
    fox_in, fox_out = fox_w_in.astype(BF16), fox_w_out.astype(BF16)
    gla_in, gla_out = gla_w_in.astype(BF16), gla_w_out.astype(BF16)
    for layer in range(depth):
        kind, j = layer % N_MIXERS, layer // N_MIXERS
        xf = _ffn(xf, norm_w[layer, 0].reshape(1, d), ffn_in, ffn_out, (layer, 0), fw, False)
        nw = norm_w[layer, 1].reshape(1, d)
        if kind == 0:
            a = _mlstm_mixer(xf, b, s, nw, mlstm_in, j, mlstm_w_in[j], mlstm_b_gate[j], mlstm_conv_w[j],
                             mlstm_norm_w[j])
            mix = (a, mlstm_out, j)
        elif kind == 1:
            a = _fox_mixer(xf, b, s, nw, fox_in, j, fox_w_in[j], fox_b_f[j], fox_qk_norm[j])
            mix = (a, fox_out, j)
        else:
            a = _gla_mixer(xf, b, s, nw, gla_in, j, gla_w_in[j], gla_w_gate_up[j], gla_b_gate[j], gla_norm_w[j])
            mix = (a, gla_out, j)
        xf = _ffn(xf, norm_w[layer, 2].reshape(1, d), ffn_in, ffn_out, (layer, 1), fw, layer == depth - 1, mix)
    return xf.reshape(b, s, d)
```

```python
import functools

import numpy as np
import jax
import jax.numpy as jnp
from jax import lax
from jax.experimental import pallas as pl
from jax.experimental.pallas import tpu as pltpu

F32 = jnp.float32
BF16 = jnp.bfloat16

EPS = 1e-6
D_MODEL = 1024
D_FF = 2816
MLSTM_HEADS = 4
MLSTM_DH = D_MODEL // MLSTM_HEADS
CONV_WIDTH = 4
FOX_HEADS = 8
FOX_DH = D_MODEL // FOX_HEADS
GLA_HEADS = 4
GLA_DK_TOTAL = D_MODEL // 2
GLA_DK = GLA_DK_TOTAL // GLA_HEADS
GLA_DV = D_MODEL // GLA_HEADS
GLA_RANK = 16
GLA_TAU = 16.0
N_MIXERS = 3

LANES = 128
SUBLANES = 8
VMEM_LIMIT_BYTES = 56 * 1024 * 1024

ROW_TILE = 512
FFN_ROW_TILE = 1024
MLSTM_CHUNK = 256
MLSTM_CHUNKS_PER_STEP = 4
MLSTM_CONV_COLS = 512
MLSTM_ROW_TILE = 512
MLSTM_ROW_GROUPS = 2
GLA_CHUNK = 128
GLA_ROW_TILE = 1024
GLA_CHUNKS_PER_STEP = 8
ATTN_TILE = 512
ATTN_HEADS_PER_STEP = 4
LOG2E = 1.4426950408889634


def _dot(a, b):
    return jnp.dot(a, b, preferred_element_type=F32)


def _dot_nt(a, b):
    return lax.dot_general(a, b, (((1,), (1,)), ((), ())), preferred_element_type=F32)


def _dot_tn(a, b):
    return lax.dot_general(a, b, (((0,), (0,)), ((), ())), preferred_element_type=F32)


def _rms(x, w):
    ms = jnp.mean(x * x, axis=-1, keepdims=True)
    return x * lax.rsqrt(ms + EPS) * w


def _sigmoid(x):
    return 1.0 / (1.0 + jnp.exp(-x))


def _log_sigmoid(x):
    return jnp.minimum(x, 0.0) - jnp.log(1.0 + jnp.exp(-jnp.abs(x)))


def _split3(x):
    hi = x.astype(BF16)
    r1 = x - hi.astype(F32)
    mid = r1.astype(BF16)
    lo = (r1 - mid.astype(F32)).astype(BF16)
    return hi, mid, lo


def _cumsum_rows(tril, x):
    hi, mid, lo = _split3(x)
    return _dot(tril, hi) + _dot(tril, mid) + _dot(tril, lo)


def _cumsum_lanes(x, triu):
    hi, mid, lo = _split3(x)
    return _dot(hi, triu) + _dot(mid, triu) + _dot(lo, triu)


def _tri(n):
    row = lax.broadcasted_iota(jnp.int32, (n, n), 0)
    col = lax.broadcasted_iota(jnp.int32, (n, n), 1)
    tril = jnp.where(row >= col, 1.0, 0.0).astype(BF16)
    triu = jnp.where(row <= col, 1.0, 0.0).astype(BF16)
    return row, col, tril, triu


def _resident(shape):
    zeros = (0,) * len(shape)
    return pl.BlockSpec(shape, lambda *_: zeros, pipeline_mode=pl.Buffered(1))


def _pick(shape, *lead):
    zeros = (0,) * len(shape)
    return pl.BlockSpec((None,) * len(lead) + tuple(shape), lambda *_: tuple(lead) + zeros,
                        pipeline_mode=pl.Buffered(1))


def _params(semantics):
    return pltpu.CompilerParams(dimension_semantics=semantics, vmem_limit_bytes=VMEM_LIMIT_BYTES)


def _row_tile(s):
    return min(ROW_TILE, s)


def _ffn_kernel(*refs, mixed, final):
    if mixed:
        x_ref, a_ref, wmix_ref, nw_ref, win_ref, wout_ref, fw_ref, o_ref = refs
        x = x_ref[...] + _dot(a_ref[...], wmix_ref[...])
    else:
        x_ref, nw_ref, win_ref, wout_ref, fw_ref, o_ref = refs
        x = x_ref[...]
    xn = _rms(x, nw_ref[...]).astype(BF16)
    g = _dot(xn, win_ref[:, :D_FF])
    u = _dot(xn, win_ref[:, D_FF:])
    a = (g * _sigmoid(g) * u).astype(BF16)
    y = x + 0.5 * _dot(a, wout_ref[...])
    if final:
        y = _rms(y, fw_ref[...])
    o_ref[...] = y


def _ffn(x, nw, w_in, w_out, sel, fw, final, mix=None):
    m = x.shape[0]
    tm = min(FFN_ROW_TILE, m)
    row = pl.BlockSpec((tm, D_MODEL), lambda i: (i, 0))
    weights = [_resident((1, D_MODEL)), _pick((D_MODEL, 2 * D_FF), *sel), _pick((D_FF, D_MODEL), *sel),
               _resident((1, D_MODEL))]
    if mix is None:
        in_specs, args = [row] + weights, (x, nw, w_in, w_out, fw)
    else:
        a, w_mix, j = mix
        in_specs = [row, row, _pick((D_MODEL, D_MODEL), j)] + weights
        args = (x, a, w_mix, nw, w_in, w_out, fw)
    return pl.pallas_call(
        functools.partial(_ffn_kernel, mixed=mix is not None, final=final),
        grid=(m // tm,),
        in_specs=in_specs,
        out_specs=row,
        out_shape=jax.ShapeDtypeStruct((m, D_MODEL), F32),
        compiler_params=_params(("parallel",)),
        name="ffn" + ("_mix" if mix is not None else "") + ("_final" if final else ""),
    )(*args)


def _mlstm_proj_kernel(x_ref, nw_ref, w_ref, wg_ref, wgt_ref, bcol_ref, brow_ref, cw_ref,
                       q_ref, k_ref, v_ref, so_ref, gcol_ref, grow_ref, carry_ref):
    tm = x_ref.shape[0]

    @pl.when(pl.program_id(1) == 0)
    def _():
        carry_ref[:SUBLANES, :] = jnp.zeros((SUBLANES, carry_ref.shape[1]), F32)

    rg = tm // MLSTM_ROW_GROUPS
    for r in range(MLSTM_ROW_GROUPS):
        rows = slice(r * rg, (r + 1) * rg)
        lo = SUBLANES + r * rg
        xn = _rms(x_ref[rows, :], nw_ref[...]).astype(BF16)
        for c in range(2 * D_MODEL // MLSTM_CONV_COLS):
            cs = slice(c * MLSTM_CONV_COLS, (c + 1) * MLSTM_CONV_COLS)
            carry_ref[lo:lo + rg, cs] = _dot(xn, w_ref[:, cs])
            cw = cw_ref[:, cs]
            conv = carry_ref[lo:lo + rg, cs] * cw[CONV_WIDTH - 1:CONV_WIDTH]
            for j in range(1, CONV_WIDTH):
                conv = conv + carry_ref[lo - j:lo - j + rg, cs] * cw[CONV_WIDTH - 1 - j:CONV_WIDTH - j]
            act = conv * _sigmoid(conv)
            if c * MLSTM_CONV_COLS < D_MODEL:
                q_ref[rows, cs] = (act * MLSTM_DH ** -0.5).astype(BF16)
            else:
                k_ref[rows, slice(cs.start - D_MODEL, cs.stop - D_MODEL)] = act.astype(BF16)
        v_ref[rows, :] = _dot(xn, w_ref[:, 2 * D_MODEL:3 * D_MODEL]).astype(BF16)
        so_ref[rows, :] = _sigmoid(_dot(xn, w_ref[:, 3 * D_MODEL:4 * D_MODEL])).astype(BF16)
        gcol_ref[rows, :] = _dot(xn, wg_ref[...]) + bcol_ref[...]
        grow_ref[:, rows] = _dot_nt(wgt_ref[...], xn) + brow_ref[...]
    carry_ref[:SUBLANES, :] = carry_ref[tm:, :]


def _select_lane(parts, sel):
    return _dot(parts[0], sel) + _dot(parts[1], sel) + _dot(parts[2], sel)


def _mlstm_core_kernel(q_ref, k_ref, v_ref, so_ref, gcol_ref, grow_ref, nw_ref, o_ref, cn_ref, m_ref):
    L = MLSTM_CHUNK if q_ref.shape[0] % MLSTM_CHUNK == 0 else q_ref.shape[0]

    @pl.when(pl.program_id(1) == 0)
    def _():
        cn_ref[...] = jnp.zeros_like(cn_ref)
        m_ref[...] = jnp.zeros_like(m_ref)

    _, _, tril, triu = _tri(L)
    row = lax.broadcasted_iota(jnp.int32, (L, LANES), 0)
    lane = lax.broadcasted_iota(jnp.int32, (L, LANES), 1)
    sel_row = lax.broadcasted_iota(jnp.int32, (LANES, LANES), 0)
    ones = jnp.ones((L, LANES), BF16)
    for c in range(q_ref.shape[0] // L):
        _mlstm_chunk(slice(c * L, (c + 1) * L), q_ref, k_ref, v_ref, so_ref, gcol_ref, grow_ref, nw_ref, o_ref,
                     cn_ref, m_ref, tril, triu, row, lane, sel_row, ones)


def _mlstm_chunk(rs, q_ref, k_ref, v_ref, so_ref, gcol_ref, grow_ref, nw_ref, o_ref, cn_ref, m_ref,
                 tril, triu, row, lane, sel_row, ones):
    L = rs.stop - rs.start
    H, DH = MLSTM_HEADS, MLSTM_DH
    nl = L // LANES
    gcol = gcol_ref[rs, :]
    grow = grow_ref[:, rs]
    bc_all = _cumsum_rows(tril, _log_sigmoid(gcol))
    br_all = _cumsum_lanes(_log_sigmoid(grow), triu)
    gcol_parts = _split3(gcol)
    bc_parts = _split3(bc_all)

    for h in range(H):
        hs = slice(h * DH, (h + 1) * DH)
        q = q_ref[rs, hs]
        k = k_ref[rs, hs]
        v_aug = jnp.concatenate([v_ref[rs, hs], ones], axis=1)
        icb = _select_lane(gcol_parts, jnp.where(sel_row == h, 1.0, 0.0).astype(BF16))
        bcb = _select_lane(bc_parts, jnp.where(sel_row == H + h, 1.0, 0.0).astype(BF16))
        a_row = grow[h:h + 1, :] - br_all[H + h:H + h + 1, :]
        m_prev = m_ref[h]
        cn_prev = cn_ref[h]

        a_tiles = [jnp.where(row >= lane + j * LANES, a_row[:, j * LANES:(j + 1) * LANES], -jnp.inf)
                   for j in range(nl)]
        a_max = a_tiles[0]
        for j in range(1, nl):
            a_max = jnp.maximum(a_max, a_tiles[j])
        u = jnp.maximum(m_prev, jnp.max(a_max, axis=-1, keepdims=True))
        inter = jnp.exp(m_prev - u)
        s = _dot_nt(q, k)
        sc = jnp.concatenate([s[:, j * LANES:(j + 1) * LANES] * jnp.exp(a_tiles[j] - u) for j in range(nl)],
                             axis=1)
        tot = _dot(sc.astype(BF16), v_aug) + jnp.concatenate([inter] * (DH // LANES + 1), axis=1) * _dot(
            q, cn_prev.astype(BF16))
        den = tot[:, DH:]
        r = 1.0 / jnp.maximum(jnp.abs(den), jnp.exp(-(bcb + u)))
        h_out = tot[:, :DH] * jnp.concatenate([r] * (DH // LANES), axis=1)

        b_last = bcb[L - 1:L, :]
        m_new = b_last + jnp.maximum(m_prev, jnp.max(a_row, axis=-1, keepdims=True))
        decay = jnp.exp(b_last + m_prev - m_new)
        w = jnp.exp(b_last - bcb + icb - m_new)
        kw = (k.astype(F32) * jnp.concatenate([w] * (DH // LANES), axis=1)).astype(BF16)
        cn_ref[h] = jnp.concatenate([decay] * (DH // LANES + 1), axis=1) * cn_prev + _dot_tn(kw, v_aug)
        m_ref[h] = m_new

        hn = _rms(h_out, nw_ref[:, hs])
        o_ref[rs, hs] = (hn * so_ref[rs, hs].astype(F32)).astype(BF16)


def _mlstm_mixer(x, b, s, nw, w_all, jm, w_in, b_gate, conv_w, norm_w):
    m = b * s
    D, H = D_MODEL, MLSTM_HEADS
    tm = min(MLSTM_ROW_TILE, s)
    nt = s // tm
    wgate = w_in[:, 4 * D:].astype(BF16)
    wg = jnp.pad(wgate, ((0, 0), (0, LANES - 2 * H)))
    wgt = wgate.T
    bcol = jnp.pad(b_gate, (0, LANES - 2 * H)).reshape(1, LANES)
    brow = b_gate.reshape(2 * H, 1)

    row = pl.BlockSpec((tm, D), lambda i, j: (i * nt + j, 0))
    q, k, v, so, gcol, grow = pl.pallas_call(
        _mlstm_proj_kernel,
        grid=(b, nt),
        in_specs=[row, _resident((1, D)), _pick((D, 4 * D + 2 * H), jm),
                  _resident((D, LANES)), _resident((2 * H, D)), _resident((1, LANES)), _resident((2 * H, 1)),
                  _resident((CONV_WIDTH, 2 * D))],
        out_specs=[row, row, row, row,
                   pl.BlockSpec((tm, LANES), lambda i, j: (i * nt + j, 0)),
                   pl.BlockSpec((2 * H, tm), lambda i, j: (0, i * nt + j))],
        out_shape=[jax.ShapeDtypeStruct((m, D), BF16)] * 4 + [
            jax.ShapeDtypeStruct((m, LANES), F32), jax.ShapeDtypeStruct((2 * H, m), F32)],
        scratch_shapes=[pltpu.VMEM((SUBLANES + tm, 2 * D), F32)],
        compiler_params=_params(("parallel", "arbitrary")),
        name="mlstm_proj",
    )(x, nw, w_all, wg, wgt, bcol, brow, conv_w)

    L = min(MLSTM_CHUNKS_PER_STEP * MLSTM_CHUNK, s)
    nc = s // L
    blk = pl.BlockSpec((L, D), lambda i, j: (i * nc + j, 0))
    hg = pl.pallas_call(
        _mlstm_core_kernel,
        grid=(b, nc),
        in_specs=[blk, blk, blk, blk,
                  pl.BlockSpec((L, LANES), lambda i, j: (i * nc + j, 0)),
                  pl.BlockSpec((2 * H, L), lambda i, j: (0, i * nc + j)),
                  _resident((1, D))],
        out_specs=blk,
        out_shape=jax.ShapeDtypeStruct((m, D), BF16),
        scratch_shapes=[pltpu.VMEM((H, MLSTM_DH, MLSTM_DH + LANES), F32), pltpu.VMEM((H, 1, LANES), F32)],
        compiler_params=_params(("parallel", "arbitrary")),
        name="mlstm_core",
    )(q, k, v, so, gcol, grow, norm_w.reshape(1, D))
    return hg


def _fox_proj_kernel(x_ref, nw_ref, w_ref, wf_ref, wft_ref, bcol_ref, brow_ref, qkn_ref,
                     q_ref, k_ref, v_ref, so_ref, fcol_ref, frow_ref, ccol_ref, crow_ref):
    tm = x_ref.shape[0]

    @pl.when(pl.program_id(1) == 0)
    def _():
        ccol_ref[...] = jnp.zeros_like(ccol_ref)
        crow_ref[...] = jnp.zeros_like(crow_ref)

    xn = _rms(x_ref[...], nw_ref[...]).astype(BF16)
    q = _dot(xn, w_ref[:, :D_MODEL])
    k = _dot(xn, w_ref[:, D_MODEL:2 * D_MODEL])
    qkn = qkn_ref[...]
    for h in range(FOX_HEADS):
        hs = slice(h * FOX_DH, (h + 1) * FOX_DH)
        q_ref[:, hs] = (_rms(q[:, hs], qkn[0:1]) * (FOX_DH ** -0.5 * LOG2E)).astype(BF16)
        k_ref[:, hs] = _rms(k[:, hs], qkn[1:2]).astype(BF16)
    v_ref[...] = _dot(xn, w_ref[:, 2 * D_MODEL:3 * D_MODEL]).astype(BF16)
    so_ref[...] = _sigmoid(_dot(xn, w_ref[:, 3 * D_MODEL:4 * D_MODEL])).astype(BF16)

    _, _, tril, triu = _tri(tm)
    lf_col = _log_sigmoid(_dot(xn, wf_ref[...]) + bcol_ref[...])
    lf_row = _log_sigmoid(_dot_nt(wft_ref[...], xn) + brow_ref[...])
    f_col = _cumsum_rows(tril, lf_col) + ccol_ref[...]
    f_row = _cumsum_lanes(lf_row, triu) + crow_ref[...]
    fcol_ref[...] = f_col * LOG2E
    for h in range(FOX_HEADS):
        frow_ref[h] = f_row[h:h + 1, :] * LOG2E
    ccol_ref[...] = f_col[tm - 1:tm, :]
    crow_ref[...] = f_row[:, tm - 1:tm]


def _fox_attn_kernel(q_ref, k_ref, v_ref, so_ref, fcol_ref, frow_ref, o_ref,
                     s_ref, m_ref, l_ref, acc_ref, fq_ref):
    t = q_ref.shape[0]
    G = s_ref.shape[0]
    nl = t // LANES
    hp = pl.program_id(1)
    qi = pl.program_id(2)

    lane = lax.broadcasted_iota(jnp.int32, (t, LANES), 1)
    row = lax.broadcasted_iota(jnp.int32, (t, LANES), 0)
    for g in range(G):
        fq = jnp.sum(jnp.where(lane == hp * G + g, fcol_ref[...], 0.0), axis=-1, keepdims=True)
        fq_ref[g] = jnp.broadcast_to(fq, (t, LANES))
    m_ref[...] = jnp.full_like(m_ref, -jnp.inf)
    l_ref[...] = jnp.zeros_like(l_ref)
    acc_ref[...] = jnp.zeros_like(acc_ref)

    def scores(kt, diagonal):
        off = pl.multiple_of(kt * t, t)
        for g in range(G):
            hs = slice(g * FOX_DH, (g + 1) * FOX_DH)
            s = _dot_nt(q_ref[:, hs], k_ref[pl.ds(off, t), hs])
            fk = frow_ref[g, :, pl.ds(off, t)]
            fqb = fq_ref[g]
            m = m_ref[g]
            for j in range(nl):
                ls = slice(j * LANES, (j + 1) * LANES)
                sj = s[:, ls] + (fqb - fk[:, ls])
                if diagonal:
                    sj = jnp.where(row >= lane + j * LANES, sj, -jnp.inf)
                s_ref[g, kt, :, ls] = sj
                m = jnp.maximum(m, sj)
            m_ref[g] = m

    def scores_body(kt, carry):
        scores(kt, False)
        return carry

    lax.fori_loop(0, qi, scores_body, 0)
    scores(qi, True)
    for g in range(G):
        m_ref[g] = jnp.broadcast_to(jnp.max(m_ref[g], axis=-1, keepdims=True), (t, LANES))

    def accumulate(kt, carry):
        off = pl.multiple_of(kt * t, t)
        for g in range(G):
            hs = slice(g * FOX_DH, (g + 1) * FOX_DH)
            mb = m_ref[g]
            l = l_ref[g]
            ps = []
            for j in range(nl):
                p = jnp.exp2(s_ref[g, kt, :, j * LANES:(j + 1) * LANES] - mb)
                l = l + p
                ps.append(p.astype(BF16))
            l_ref[g] = l
            acc_ref[g] += _dot(jnp.concatenate(ps, axis=1), v_ref[pl.ds(off, t), hs])
        return carry

    lax.fori_loop(0, qi + 1, accumulate, 0)
    for g in range(G):
        hs = slice(g * FOX_DH, (g + 1) * FOX_DH)
        l = jnp.sum(l_ref[g], axis=-1, keepdims=True)
        o_ref[:, hs] = (acc_ref[g] * (1.0 / l) * so_ref[:, hs].astype(F32)).astype(BF16)


def _fox_mixer(x, b, s, nw, w_all, jm, w_in, b_f, qk_norm):
    m = b * s
    D, H, DH = D_MODEL, FOX_HEADS, FOX_DH
    tm = _row_tile(s)
    nt = s // tm
    wgate = w_in[:, 4 * D:].astype(BF16)
    wf = jnp.pad(wgate, ((0, 0), (0, LANES - H)))
    wft = wgate.T
    bcol = jnp.pad(b_f, (0, LANES - H)).reshape(1, LANES)
    brow = b_f.reshape(H, 1)

    row = pl.BlockSpec((tm, D), lambda i, j: (i * nt + j, 0))
    q, k, v, so, fcol, frow = pl.pallas_call(
        _fox_proj_kernel,
        grid=(b, nt),
        in_specs=[row, _resident((1, D)), _pick((D, 4 * D + H), jm), _resident((D, LANES)), _resident((H, D)),
                  _resident((1, LANES)), _resident((H, 1)), _resident((2, DH))],
        out_specs=[row, row, row, row,
                   pl.BlockSpec((tm, LANES), lambda i, j: (i * nt + j, 0)),
                   pl.BlockSpec((H, 1, tm), lambda i, j: (0, 0, i * nt + j))],
        out_shape=[jax.ShapeDtypeStruct((m, D), BF16)] * 4 + [
            jax.ShapeDtypeStruct((m, LANES), F32), jax.ShapeDtypeStruct((H, 1, m), F32)],
        scratch_shapes=[pltpu.VMEM((1, LANES), F32), pltpu.VMEM((H, 1), F32)],
        compiler_params=_params(("parallel", "arbitrary")),
        name="fox_proj",
    )(x, nw, w_all, wf, wft, bcol, brow, qk_norm)

    t = min(ATTN_TILE, s)
    nq = s // t
    G = ATTN_HEADS_PER_STEP
    qblk = pl.BlockSpec((t, G * DH), lambda bi, hi, qi: (bi * nq + qi, hi))
    kblk = pl.BlockSpec((s, G * DH), lambda bi, hi, qi: (bi, hi))
    att = pl.pallas_call(
        _fox_attn_kernel,
        grid=(b, H // G, nq),
        in_specs=[qblk, kblk, kblk, qblk,
                  pl.BlockSpec((t, LANES), lambda bi, hi, qi: (bi * nq + qi, 0)),
                  pl.BlockSpec((G, 1, s), lambda bi, hi, qi: (hi, 0, bi))],
        out_specs=qblk,
        out_shape=jax.ShapeDtypeStruct((m, D), BF16),
        scratch_shapes=[pltpu.VMEM((G, nq, t, t), F32), pltpu.VMEM((G, t, LANES), F32),
                        pltpu.VMEM((G, t, LANES), F32), pltpu.VMEM((G, t, DH), F32),
                        pltpu.VMEM((G, t, LANES), F32)],
        compiler_params=_params(("parallel", "parallel", "arbitrary")),
        name="fox_attn",
    )(q, k, v, so, fcol, frow)
    return att


def _gla_level_matrix(L, levels):
    t = np.arange(L)[:, None]
    c = np.arange(L)[None, :]
    blocks = [(c <= t)]
    for j in range(levels):
        hs = 1 << j
        mid = (t & ~(2 * hs - 1)) + hs
        upper = (t & hs) != 0
        blocks.append(np.where(upper, (c > mid) & (c <= t), (c > t) & (c <= mid)))
    return np.concatenate(blocks, axis=0).astype(np.float32)


def _gla_proj_kernel(x_ref, nw_ref, w_ref, wg_ref, wup_ref, bg_ref,
                     q_ref, k_ref, v_ref, sr_ref, la_ref):
    xn = _rms(x_ref[...], nw_ref[...]).astype(BF16)
    DKT = GLA_DK_TOTAL
    q_ref[...] = (_dot(xn, w_ref[:, :DKT]) * GLA_DK ** -0.5).astype(BF16)
    k_ref[...] = _dot(xn, w_ref[:, DKT:2 * DKT]).astype(BF16)
    v_ref[...] = _dot(xn, w_ref[:, 2 * DKT:2 * DKT + D_MODEL]).astype(BF16)
    r = _dot(xn, w_ref[:, 2 * DKT + D_MODEL:2 * DKT + 2 * D_MODEL])
    sr_ref[...] = (r * _sigmoid(r)).astype(BF16)
    glr = _dot(xn, wg_ref[...])
    z = _dot(glr.astype(BF16), wup_ref[...]) + bg_ref[...]
    la_ref[...] = _log_sigmoid(z) * (1.0 / GLA_TAU)


def _gla_core_kernel(q_ref, k_ref, v_ref, sr_ref, la_ref, lvl_ref, nw_ref, o_ref, st_ref):
    L = GLA_CHUNK if q_ref.shape[0] % GLA_CHUNK == 0 else q_ref.shape[0]
    H, DK, DV = GLA_HEADS, GLA_DK, GLA_DV
    levels = lvl_ref.shape[0] // L - 1

    @pl.when(pl.program_id(1) == 0)
    def _():
        st_ref[...] = jnp.zeros_like(st_ref)

    row = lax.broadcasted_iota(jnp.int32, (L, L), 0)
    col = lax.broadcasted_iota(jnp.int32, (L, L), 1)
    diff = jnp.where(row > col, row ^ col, 0)

    for c in range(q_ref.shape[0] // L):
        rs = slice(c * L, (c + 1) * L)
        hi, mid, _ = _split3(la_ref[rs, :])
        args = _dot(lvl_ref[...], jnp.concatenate([hi, mid], axis=0))
        for h in range(H):
            ks = slice(h * DK, (h + 1) * DK)
            vs = slice(h * DV, (h + 1) * DV)
            q = q_ref[rs, ks].astype(F32)
            k = k_ref[rs, ks].astype(F32)
            v = v_ref[rs, vs]
            g = args[0:L, ks]
            a = jnp.where(row == col, jnp.sum(q * k, axis=-1, keepdims=True), 0.0)
            for j in range(levels):
                e = jnp.exp(args[(j + 1) * L:(j + 2) * L, ks])
                aj = _dot_nt((q * e).astype(BF16), (k * e).astype(BF16))
                a = jnp.where((diff >> j) == 1, aj, a)
            st = st_ref[h]
            o = _dot(a.astype(BF16), v) + _dot_nt((q * jnp.exp(g)).astype(BF16), st.astype(BF16))
            g_last = g[L - 1:L, :]
            kd = (k * jnp.exp(g_last - g)).astype(BF16)
            st_ref[h] = st * jnp.exp(g_last) + _dot_tn(v, kd)
            on = _rms(o, nw_ref[:, vs])
            o_ref[rs, vs] = (on * sr_ref[rs, vs].astype(F32)).astype(BF16)


def _gla_mixer(x, b, s, nw, w_all, jm, w_in, w_gate_up, b_gate, norm_w):
    m = b * s
    D, H, DKT, R = D_MODEL, GLA_HEADS, GLA_DK_TOTAL, GLA_RANK
    tm = min(GLA_ROW_TILE, m)
    wg = jnp.pad(w_in[:, 2 * DKT + 2 * D:].astype(BF16), ((0, 0), (0, LANES - R)))
    wup = jnp.pad(w_gate_up.astype(BF16), ((0, LANES - R), (0, 0)))

    row = pl.BlockSpec((tm, D), lambda i: (i, 0))
    half = pl.BlockSpec((tm, DKT), lambda i: (i, 0))
    q, k, v, sr, la = pl.pallas_call(
        _gla_proj_kernel,
        grid=(m // tm,),
        in_specs=[row, _resident((1, D)), _pick((D, 2 * DKT + 2 * D + R), jm), _resident((D, LANES)),
                  _resident((LANES, DKT)), _resident((1, DKT))],
        out_specs=[half, half, row, row, half],
        out_shape=[jax.ShapeDtypeStruct((m, DKT), BF16), jax.ShapeDtypeStruct((m, DKT), BF16),
                   jax.ShapeDtypeStruct((m, D), BF16), jax.ShapeDtypeStruct((m, D), BF16),
                   jax.ShapeDtypeStruct((m, DKT), F32)],
        compiler_params=_params(("parallel",)),
        name="gla_proj",
    )(x, nw, w_all, wg, wup, b_gate.reshape(1, DKT))

    L = min(GLA_CHUNK, s)
    levels = L.bit_length() - 1
    rows = min(GLA_CHUNKS_PER_STEP * L, s)
    nc = s // rows
    lvl = _gla_level_matrix(L, levels)
    lvl = jnp.asarray(np.concatenate([lvl, lvl], axis=1), dtype=BF16)
    kblk = pl.BlockSpec((rows, DKT), lambda i, j: (i * nc + j, 0))
    vblk = pl.BlockSpec((rows, D), lambda i, j: (i * nc + j, 0))
    og = pl.pallas_call(
        _gla_core_kernel,
        grid=(b, nc),
        in_specs=[kblk, kblk, vblk, vblk, kblk, _resident(((levels + 1) * L, 2 * L)), _resident((1, D))],
        out_specs=vblk,
        out_shape=jax.ShapeDtypeStruct((m, D), BF16),
        scratch_shapes=[pltpu.VMEM((H, GLA_DV, GLA_DK), F32)],
        compiler_params=_params(("parallel", "arbitrary")),
        name="gla_core",
    )(q, k, v, sr, la, lvl, norm_w.reshape(1, D))
    return og


def kernel(x, norm_w, ffn_w_in, ffn_w_out, mlstm_w_in, mlstm_b_gate, mlstm_conv_w, mlstm_norm_w, mlstm_w_out,
           fox_w_in, fox_b_f, fox_qk_norm, fox_w_out, gla_w_in, gla_w_gate_up, gla_b_gate, gla_norm_w,
           gla_w_out, final_norm_w):
    b, s, d = x.shape
    depth = norm_w.shape[0]
    xf = x.reshape(b * s, d)
    fw = final_norm_w.reshape(1, d)
    ffn_in, ffn_out = ffn_w_in.astype(BF16), ffn_w_out.astype(BF16)
    mlstm_in, mlstm_out = mlstm_w_in.astype(BF16), mlstm_w_out.astype(BF16)
    fox_in, fox_out = fox_w_in.astype(BF16), fox_w_out.astype(BF16)
    gla_in, gla_out = gla_w_in.astype(BF16), gla_w_out.astype(BF16)
    for layer in range(depth):
        kind, j = layer % N_MIXERS, layer // N_MIXERS
        xf = _ffn(xf, norm_w[layer, 0].reshape(1, d), ffn_in, ffn_out, (layer, 0), fw, False)
        nw = norm_w[layer, 1].reshape(1, d)
        if kind == 0:
            a = _mlstm_mixer(xf, b, s, nw, mlstm_in, j, mlstm_w_in[j], mlstm_b_gate[j], mlstm_conv_w[j],
                             mlstm_norm_w[j])
            mix = (a, mlstm_out, j)
        elif kind == 1:
            a = _fox_mixer(xf, b, s, nw, fox_in, j, fox_w_in[j], fox_b_f[j], fox_qk_norm[j])
            mix = (a, fox_out, j)
        else:
            a = _gla_mixer(xf, b, s, nw, gla_in, j, gla_w_in[j], gla_w_gate_up[j], gla_b_gate[j], gla_norm_w[j])
            mix = (a, gla_out, j)
        xf = _ffn(xf, norm_w[layer, 2].reshape(1, d), ffn_in, ffn_out, (layer, 1), fw, layer == depth - 1, mix)
    return xf.reshape(b, s, d)
```

```python
import functools

import numpy as np
import jax
import jax.numpy as jnp
from jax import lax
from jax.experimental import pallas as pl
from jax.experimental.pallas import tpu as pltpu

F32 = jnp.float32
BF16 = jnp.bfloat16

EPS = 1e-6
D_MODEL = 1024
D_FF = 2816
MLSTM_HEADS = 4
MLSTM_DH = D_MODEL // MLSTM_HEADS
CONV_WIDTH = 4
FOX_HEADS = 8
FOX_DH = D_MODEL // FOX_HEADS
GLA_HEADS = 4
GLA_DK_TOTAL = D_MODEL // 2
GLA_DK = GLA_DK_TOTAL // GLA_HEADS
GLA_DV = D_MODEL // GLA_HEADS
GLA_RANK = 16
GLA_TAU = 16.0
N_MIXERS = 3

LANES = 128
SUBLANES = 8
VMEM_LIMIT_BYTES = 56 * 1024 * 1024

ROW_TILE = 512
FFN_ROW_TILE = 1024
MLSTM_CHUNK = 256
MLSTM_CHUNKS_PER_STEP = 4
MLSTM_CONV_COLS = 512
MLSTM_ROW_TILE = 512
MLSTM_ROW_GROUPS = 2
GLA_CHUNK = 128
GLA_ROW_TILE = 1024
GLA_CHUNKS_PER_STEP = 8
FOX_ROW_TILE = 1024
FOX_ROW_GROUPS = 4
ATTN_TILE = 512
ATTN_HEADS_PER_STEP = 4
LOG2E = 1.4426950408889634


def _dot(a, b):
    return jnp.dot(a, b, preferred_element_type=F32)


def _dot_nt(a, b):
    return lax.dot_general(a, b, (((1,), (1,)), ((), ())), preferred_element_type=F32)


def _dot_tn(a, b):
    return lax.dot_general(a, b, (((0,), (0,)), ((), ())), preferred_element_type=F32)


def _rms(x, w):
    ms = jnp.mean(x * x, axis=-1, keepdims=True)
    return x * lax.rsqrt(ms + EPS) * w


def _sigmoid(x):
    return 1.0 / (1.0 + jnp.exp(-x))


def _log_sigmoid(x):
    return jnp.minimum(x, 0.0) - jnp.log(1.0 + jnp.exp(-jnp.abs(x)))


def _split3(x):
    hi = x.astype(BF16)
    r1 = x - hi.astype(F32)
    mid = r1.astype(BF16)
    lo = (r1 - mid.astype(F32)).astype(BF16)
    return hi, mid, lo


def _cumsum_rows(tril, x):
    hi, mid, lo = _split3(x)
    return _dot(tril, hi) + _dot(tril, mid) + _dot(tril, lo)


def _cumsum_lanes(x, triu):
    hi, mid, lo = _split3(x)
    return _dot(hi, triu) + _dot(mid, triu) + _dot(lo, triu)


def _tri(n):
    row = lax.broadcasted_iota(jnp.int32, (n, n), 0)
    col = lax.broadcasted_iota(jnp.int32, (n, n), 1)
    tril = jnp.where(row >= col, 1.0, 0.0).astype(BF16)
    triu = jnp.where(row <= col, 1.0, 0.0).astype(BF16)
    return row, col, tril, triu


def _resident(shape):
    zeros = (0,) * len(shape)
    return pl.BlockSpec(shape, lambda *_: zeros, pipeline_mode=pl.Buffered(1))


def _pick(shape, *lead):
    zeros = (0,) * len(shape)
    return pl.BlockSpec((None,) * len(lead) + tuple(shape), lambda *_: tuple(lead) + zeros,
                        pipeline_mode=pl.Buffered(1))


def _params(semantics):
    return pltpu.CompilerParams(dimension_semantics=semantics, vmem_limit_bytes=VMEM_LIMIT_BYTES)


def _row_tile(s):
    return min(ROW_TILE, s)


def _ffn_kernel(*refs, mixed, final):
    if mixed:
        x_ref, a_ref, wmix_ref, nw_ref, win_ref, wout_ref, fw_ref, o_ref = refs
        x = x_ref[...] + _dot(a_ref[...], wmix_ref[...])
    else:
        x_ref, nw_ref, win_ref, wout_ref, fw_ref, o_ref = refs
        x = x_ref[...]
    xn = _rms(x, nw_ref[...]).astype(BF16)
    g = _dot(xn, win_ref[:, :D_FF])
    u = _dot(xn, win_ref[:, D_FF:])
    a = (g * _sigmoid(g) * u).astype(BF16)
    y = x + 0.5 * _dot(a, wout_ref[...])
    if final:
        y = _rms(y, fw_ref[...])
    o_ref[...] = y


def _ffn(x, nw, w_in, w_out, sel, fw, final, mix=None):
    m = x.shape[0]
    tm = min(FFN_ROW_TILE, m)
    row = pl.BlockSpec((tm, D_MODEL), lambda i: (i, 0))
    weights = [_resident((1, D_MODEL)), _pick((D_MODEL, 2 * D_FF), *sel), _pick((D_FF, D_MODEL), *sel),
               _resident((1, D_MODEL))]
    if mix is None:
        in_specs, args = [row] + weights, (x, nw, w_in, w_out, fw)
    else:
        a, w_mix, j = mix
        in_specs = [row, row, _pick((D_MODEL, D_MODEL), j)] + weights
        args = (x, a, w_mix, nw, w_in, w_out, fw)
    return pl.pallas_call(
        functools.partial(_ffn_kernel, mixed=mix is not None, final=final),
        grid=(m // tm,),
        in_specs=in_specs,
        out_specs=row,
        out_shape=jax.ShapeDtypeStruct((m, D_MODEL), F32),
        compiler_params=_params(("parallel",)),
        name="ffn" + ("_mix" if mix is not None else "") + ("_final" if final else ""),
    )(*args)


def _mlstm_proj_kernel(x_ref, nw_ref, w_ref, wg_ref, wgt_ref, bcol_ref, brow_ref, cw_ref,
                       q_ref, k_ref, v_ref, so_ref, gcol_ref, grow_ref, carry_ref):
    tm = x_ref.shape[0]

    @pl.when(pl.program_id(1) == 0)
    def _():
        carry_ref[:SUBLANES, :] = jnp.zeros((SUBLANES, carry_ref.shape[1]), F32)

    rg = tm // MLSTM_ROW_GROUPS
    for r in range(MLSTM_ROW_GROUPS):
        rows = slice(r * rg, (r + 1) * rg)
        lo = SUBLANES + r * rg
        xn = _rms(x_ref[rows, :], nw_ref[...]).astype(BF16)
        for c in range(2 * D_MODEL // MLSTM_CONV_COLS):
            cs = slice(c * MLSTM_CONV_COLS, (c + 1) * MLSTM_CONV_COLS)
            carry_ref[lo:lo + rg, cs] = _dot(xn, w_ref[:, cs])
            cw = cw_ref[:, cs]
            conv = carry_ref[lo:lo + rg, cs] * cw[CONV_WIDTH - 1:CONV_WIDTH]
            for j in range(1, CONV_WIDTH):
                conv = conv + carry_ref[lo - j:lo - j + rg, cs] * cw[CONV_WIDTH - 1 - j:CONV_WIDTH - j]
            act = conv * _sigmoid(conv)
            if c * MLSTM_CONV_COLS < D_MODEL:
                q_ref[rows, cs] = (act * MLSTM_DH ** -0.5).astype(BF16)
            else:
                k_ref[rows, slice(cs.start - D_MODEL, cs.stop - D_MODEL)] = act.astype(BF16)
        v_ref[rows, :] = _dot(xn, w_ref[:, 2 * D_MODEL:3 * D_MODEL]).astype(BF16)
        so_ref[rows, :] = _sigmoid(_dot(xn, w_ref[:, 3 * D_MODEL:4 * D_MODEL])).astype(BF16)
        gcol_ref[rows, :] = _dot(xn, wg_ref[...]) + bcol_ref[...]
        grow_ref[:, rows] = _dot_nt(wgt_ref[...], xn) + brow_ref[...]
    carry_ref[:SUBLANES, :] = carry_ref[tm:, :]


def _select_lane(parts, sel):
    return _dot(parts[0], sel) + _dot(parts[1], sel) + _dot(parts[2], sel)


def _mlstm_core_kernel(q_ref, k_ref, v_ref, so_ref, gcol_ref, grow_ref, nw_ref, o_ref, cn_ref, m_ref):
    L = MLSTM_CHUNK if q_ref.shape[0] % MLSTM_CHUNK == 0 else q_ref.shape[0]

    @pl.when(pl.program_id(1) == 0)
    def _():
        cn_ref[...] = jnp.zeros_like(cn_ref)
        m_ref[...] = jnp.zeros_like(m_ref)

    _, _, tril, triu = _tri(L)
    row = lax.broadcasted_iota(jnp.int32, (L, LANES), 0)
    lane = lax.broadcasted_iota(jnp.int32, (L, LANES), 1)
    sel_row = lax.broadcasted_iota(jnp.int32, (LANES, LANES), 0)
    ones = jnp.ones((L, LANES), BF16)
    for c in range(q_ref.shape[0] // L):
        _mlstm_chunk(slice(c * L, (c + 1) * L), q_ref, k_ref, v_ref, so_ref, gcol_ref, grow_ref, nw_ref, o_ref,
                     cn_ref, m_ref, tril, triu, row, lane, sel_row, ones)


def _mlstm_chunk(rs, q_ref, k_ref, v_ref, so_ref, gcol_ref, grow_ref, nw_ref, o_ref, cn_ref, m_ref,
                 tril, triu, row, lane, sel_row, ones):
    L = rs.stop - rs.start
    H, DH = MLSTM_HEADS, MLSTM_DH
    nl = L // LANES
    gcol = gcol_ref[rs, :]
    grow = grow_ref[:, rs]
    bc_all = _cumsum_rows(tril, _log_sigmoid(gcol))
    br_all = _cumsum_lanes(_log_sigmoid(grow), triu)
    gcol_parts = _split3(gcol)
    bc_parts = _split3(bc_all)

    for h in range(H):
        hs = slice(h * DH, (h + 1) * DH)
        q = q_ref[rs, hs]
        k = k_ref[rs, hs]
        v_aug = jnp.concatenate([v_ref[rs, hs], ones], axis=1)
        icb = _select_lane(gcol_parts, jnp.where(sel_row == h, 1.0, 0.0).astype(BF16))
        bcb = _select_lane(bc_parts, jnp.where(sel_row == H + h, 1.0, 0.0).astype(BF16))
        a_row = grow[h:h + 1, :] - br_all[H + h:H + h + 1, :]
        m_prev = m_ref[h]
        cn_prev = cn_ref[h]

        a_tiles = [jnp.where(row >= lane + j * LANES, a_row[:, j * LANES:(j + 1) * LANES], -jnp.inf)
                   for j in range(nl)]
        a_max = a_tiles[0]
        for j in range(1, nl):
            a_max = jnp.maximum(a_max, a_tiles[j])
        u = jnp.maximum(m_prev, jnp.max(a_max, axis=-1, keepdims=True))
        inter = jnp.exp(m_prev - u)
        s = _dot_nt(q, k)
        sc = jnp.concatenate([s[:, j * LANES:(j + 1) * LANES] * jnp.exp(a_tiles[j] - u) for j in range(nl)],
                             axis=1)
        tot = _dot(sc.astype(BF16), v_aug) + jnp.concatenate([inter] * (DH // LANES + 1), axis=1) * _dot(
            q, cn_prev.astype(BF16))
        den = tot[:, DH:]
        r = 1.0 / jnp.maximum(jnp.abs(den), jnp.exp(-(bcb + u)))
        h_out = tot[:, :DH] * jnp.concatenate([r] * (DH // LANES), axis=1)

        b_last = bcb[L - 1:L, :]
        m_new = b_last + jnp.maximum(m_prev, jnp.max(a_row, axis=-1, keepdims=True))
        decay = jnp.exp(b_last + m_prev - m_new)
        w = jnp.exp(b_last - bcb + icb - m_new)
        kw = (k.astype(F32) * jnp.concatenate([w] * (DH // LANES), axis=1)).astype(BF16)
        cn_ref[h] = jnp.concatenate([decay] * (DH // LANES + 1), axis=1) * cn_prev + _dot_tn(kw, v_aug)
        m_ref[h] = m_new

        hn = _rms(h_out, nw_ref[:, hs])
        o_ref[rs, hs] = (hn * so_ref[rs, hs].astype(F32)).astype(BF16)


def _mlstm_mixer(x, b, s, nw, w_all, jm, w_in, b_gate, conv_w, norm_w):
    m = b * s
    D, H = D_MODEL, MLSTM_HEADS
    tm = min(MLSTM_ROW_TILE, s)
    nt = s // tm
    wgate = w_in[:, 4 * D:].astype(BF16)
    wg = jnp.pad(wgate, ((0, 0), (0, LANES - 2 * H)))
    wgt = wgate.T
    bcol = jnp.pad(b_gate, (0, LANES - 2 * H)).reshape(1, LANES)
    brow = b_gate.reshape(2 * H, 1)

    row = pl.BlockSpec((tm, D), lambda i, j: (i * nt + j, 0))
    q, k, v, so, gcol, grow = pl.pallas_call(
        _mlstm_proj_kernel,
        grid=(b, nt),
        in_specs=[row, _resident((1, D)), _pick((D, 4 * D + 2 * H), jm),
                  _resident((D, LANES)), _resident((2 * H, D)), _resident((1, LANES)), _resident((2 * H, 1)),
                  _resident((CONV_WIDTH, 2 * D))],
        out_specs=[row, row, row, row,
                   pl.BlockSpec((tm, LANES), lambda i, j: (i * nt + j, 0)),
                   pl.BlockSpec((2 * H, tm), lambda i, j: (0, i * nt + j))],
        out_shape=[jax.ShapeDtypeStruct((m, D), BF16)] * 4 + [
            jax.ShapeDtypeStruct((m, LANES), F32), jax.ShapeDtypeStruct((2 * H, m), F32)],
        scratch_shapes=[pltpu.VMEM((SUBLANES + tm, 2 * D), F32)],
        compiler_params=_params(("parallel", "arbitrary")),
        name="mlstm_proj",
    )(x, nw, w_all, wg, wgt, bcol, brow, conv_w)

    L = min(MLSTM_CHUNKS_PER_STEP * MLSTM_CHUNK, s)
    nc = s // L
    blk = pl.BlockSpec((L, D), lambda i, j: (i * nc + j, 0))
    hg = pl.pallas_call(
        _mlstm_core_kernel,
        grid=(b, nc),
        in_specs=[blk, blk, blk, blk,
                  pl.BlockSpec((L, LANES), lambda i, j: (i * nc + j, 0)),
                  pl.BlockSpec((2 * H, L), lambda i, j: (0, i * nc + j)),
                  _resident((1, D))],
        out_specs=blk,
        out_shape=jax.ShapeDtypeStruct((m, D), BF16),
        scratch_shapes=[pltpu.VMEM((H, MLSTM_DH, MLSTM_DH + LANES), F32), pltpu.VMEM((H, 1, LANES), F32)],
        compiler_params=_params(("parallel", "arbitrary")),
        name="mlstm_core",
    )(q, k, v, so, gcol, grow, norm_w.reshape(1, D))
    return hg


def _fox_proj_kernel(x_ref, nw_ref, w_ref, wf_ref, wft_ref, bcol_ref, brow_ref, qkn_ref,
                     q_ref, k_ref, v_ref, so_ref, fcol_ref, frow_ref, ccol_ref, crow_ref):
    tm = x_ref.shape[0]

    @pl.when(pl.program_id(1) == 0)
    def _():
        ccol_ref[...] = jnp.zeros_like(ccol_ref)
        crow_ref[...] = jnp.zeros_like(crow_ref)

    rg = tm // FOX_ROW_GROUPS
    _, _, tril, triu = _tri(rg)
    qkn = qkn_ref[...]
    for r in range(FOX_ROW_GROUPS):
        rows = slice(r * rg, (r + 1) * rg)
        xn = _rms(x_ref[rows, :], nw_ref[...]).astype(BF16)
        q = _dot(xn, w_ref[:, :D_MODEL])
        k = _dot(xn, w_ref[:, D_MODEL:2 * D_MODEL])
        for h in range(FOX_HEADS):
            hs = slice(h * FOX_DH, (h + 1) * FOX_DH)
            q_ref[rows, hs] = (_rms(q[:, hs], qkn[0:1]) * (FOX_DH ** -0.5 * LOG2E)).astype(BF16)
            k_ref[rows, hs] = _rms(k[:, hs], qkn[1:2]).astype(BF16)
        v_ref[rows, :] = _dot(xn, w_ref[:, 2 * D_MODEL:3 * D_MODEL]).astype(BF16)
        so_ref[rows, :] = _sigmoid(_dot(xn, w_ref[:, 3 * D_MODEL:4 * D_MODEL])).astype(BF16)

        lf_col = _log_sigmoid(_dot(xn, wf_ref[...]) + bcol_ref[...])
        lf_row = _log_sigmoid(_dot_nt(wft_ref[...], xn) + brow_ref[...])
        f_col = _cumsum_rows(tril, lf_col) + ccol_ref[...]
        f_row = _cumsum_lanes(lf_row, triu) + crow_ref[...]
        fcol_ref[rows, :] = f_col * LOG2E
        for h in range(FOX_HEADS):
            frow_ref[h, :, rows] = f_row[h:h + 1, :] * LOG2E
        ccol_ref[...] = f_col[rg - 1:rg, :]
        crow_ref[...] = f_row[:, rg - 1:rg]


def _fox_attn_kernel(q_ref, k_ref, v_ref, so_ref, fcol_ref, frow_ref, o_ref,
                     s_ref, m_ref, l_ref, acc_ref, fq_ref):
    t = q_ref.shape[0]
    G = s_ref.shape[0]
    nl = t // LANES
    hp = pl.program_id(1)
    qi = pl.program_id(2)

    lane = lax.broadcasted_iota(jnp.int32, (t, LANES), 1)
    row = lax.broadcasted_iota(jnp.int32, (t, LANES), 0)
    for g in range(G):
        fq = jnp.sum(jnp.where(lane == hp * G + g, fcol_ref[...], 0.0), axis=-1, keepdims=True)
        fq_ref[g] = jnp.broadcast_to(fq, (t, LANES))
    m_ref[...] = jnp.full_like(m_ref, -jnp.inf)
    l_ref[...] = jnp.zeros_like(l_ref)
    acc_ref[...] = jnp.zeros_like(acc_ref)

    def scores(kt, diagonal):
        off = pl.multiple_of(kt * t, t)
        for g in range(G):
            hs = slice(g * FOX_DH, (g + 1) * FOX_DH)
            s = _dot_nt(q_ref[:, hs], k_ref[pl.ds(off, t), hs])
            fk = frow_ref[g, :, pl.ds(off, t)]
            fqb = fq_ref[g]
            m = m_ref[g]
            for j in range(nl):
                ls = slice(j * LANES, (j + 1) * LANES)
                sj = s[:, ls] + (fqb - fk[:, ls])
                if diagonal:
                    sj = jnp.where(row >= lane + j * LANES, sj, -jnp.inf)
                s_ref[g, kt, :, ls] = sj
                m = jnp.maximum(m, sj)
            m_ref[g] = m

    def scores_body(kt, carry):
        scores(kt, False)
        return carry

    lax.fori_loop(0, qi, scores_body, 0)
    scores(qi, True)
    for g in range(G):
        m_ref[g] = jnp.broadcast_to(jnp.max(m_ref[g], axis=-1, keepdims=True), (t, LANES))

    def accumulate(kt, carry):
        off = pl.multiple_of(kt * t, t)
        for g in range(G):
            hs = slice(g * FOX_DH, (g + 1) * FOX_DH)
            mb = m_ref[g]
            l = l_ref[g]
            ps = []
            for j in range(nl):
                p = jnp.exp2(s_ref[g, kt, :, j * LANES:(j + 1) * LANES] - mb)
                l = l + p
                ps.append(p.astype(BF16))
            l_ref[g] = l
            acc_ref[g] += _dot(jnp.concatenate(ps, axis=1), v_ref[pl.ds(off, t), hs])
        return carry

    lax.fori_loop(0, qi + 1, accumulate, 0)
    for g in range(G):
        hs = slice(g * FOX_DH, (g + 1) * FOX_DH)
        l = jnp.sum(l_ref[g], axis=-1, keepdims=True)
        o_ref[:, hs] = (acc_ref[g] * (1.0 / l) * so_ref[:, hs].astype(F32)).astype(BF16)


def _fox_mixer(x, b, s, nw, w_all, jm, w_in, b_f, qk_norm):
    m = b * s
    D, H, DH = D_MODEL, FOX_HEADS, FOX_DH
    tm = min(FOX_ROW_TILE, s)
    nt = s // tm
    wgate = w_in[:, 4 * D:].astype(BF16)
    wf = jnp.pad(wgate, ((0, 0), (0, LANES - H)))
    wft = wgate.T
    bcol = jnp.pad(b_f, (0, LANES - H)).reshape(1, LANES)
    brow = b_f.reshape(H, 1)

    row = pl.BlockSpec((tm, D), lambda i, j: (i * nt + j, 0))
    q, k, v, so, fcol, frow = pl.pallas_call(
        _fox_proj_kernel,
        grid=(b, nt),
        in_specs=[row, _resident((1, D)), _pick((D, 4 * D + H), jm), _resident((D, LANES)), _resident((H, D)),
                  _resident((1, LANES)), _resident((H, 1)), _resident((2, DH))],
        out_specs=[row, row, row, row,
                   pl.BlockSpec((tm, LANES), lambda i, j: (i * nt + j, 0)),
                   pl.BlockSpec((H, 1, tm), lambda i, j: (0, 0, i * nt + j))],
        out_shape=[jax.ShapeDtypeStruct((m, D), BF16)] * 4 + [
            jax.ShapeDtypeStruct((m, LANES), F32), jax.ShapeDtypeStruct((H, 1, m), F32)],
        scratch_shapes=[pltpu.VMEM((1, LANES), F32), pltpu.VMEM((H, 1), F32)],
        compiler_params=_params(("parallel", "arbitrary")),
        name="fox_proj",
    )(x, nw, w_all, wf, wft, bcol, brow, qk_norm)

    t = min(ATTN_TILE, s)
    nq = s // t
    G = ATTN_HEADS_PER_STEP
    qblk = pl.BlockSpec((t, G * DH), lambda bi, hi, qi: (bi * nq + qi, hi))
    kblk = pl.BlockSpec((s, G * DH), lambda bi, hi, qi: (bi, hi))
    att = pl.pallas_call(
        _fox_attn_kernel,
        grid=(b, H // G, nq),
        in_specs=[qblk, kblk, kblk, qblk,
                  pl.BlockSpec((t, LANES), lambda bi, hi, qi: (bi * nq + qi, 0)),
                  pl.BlockSpec((G, 1, s), lambda bi, hi, qi: (hi, 0, bi))],
        out_specs=qblk,
        out_shape=jax.ShapeDtypeStruct((m, D), BF16),
        scratch_shapes=[pltpu.VMEM((G, nq, t, t), F32), pltpu.VMEM((G, t, LANES), F32),
                        pltpu.VMEM((G, t, LANES), F32), pltpu.VMEM((G, t, DH), F32),
                        pltpu.VMEM((G, t, LANES), F32)],
        compiler_params=_params(("parallel", "parallel", "arbitrary")),
        name="fox_attn",
    )(q, k, v, so, fcol, frow)
    return att


def _gla_level_matrix(L, levels):
    t = np.arange(L)[:, None]
    c = np.arange(L)[None, :]
    blocks = [(c <= t)]
    for j in range(levels):
        hs = 1 << j
        mid = (t & ~(2 * hs - 1)) + hs
        upper = (t & hs) != 0
        blocks.append(np.where(upper, (c > mid) & (c <= t), (c > t) & (c <= mid)))
    return np.concatenate(blocks, axis=0).astype(np.float32)


def _gla_proj_kernel(x_ref, nw_ref, w_ref, wg_ref, wup_ref, bg_ref,
                     q_ref, k_ref, v_ref, sr_ref, la_ref):
    xn = _rms(x_ref[...], nw_ref[...]).astype(BF16)
    DKT = GLA_DK_TOTAL
    q_ref[...] = (_dot(xn, w_ref[:, :DKT]) * GLA_DK ** -0.5).astype(BF16)
    k_ref[...] = _dot(xn, w_ref[:, DKT:2 * DKT]).astype(BF16)
    v_ref[...] = _dot(xn, w_ref[:, 2 * DKT:2 * DKT + D_MODEL]).astype(BF16)
    r = _dot(xn, w_ref[:, 2 * DKT + D_MODEL:2 * DKT + 2 * D_MODEL])
    sr_ref[...] = (r * _sigmoid(r)).astype(BF16)
    glr = _dot(xn, wg_ref[...])
    z = _dot(glr.astype(BF16), wup_ref[...]) + bg_ref[...]
    la_ref[...] = _log_sigmoid(z) * (1.0 / GLA_TAU)


def _gla_core_kernel(q_ref, k_ref, v_ref, sr_ref, la_ref, lvl_ref, nw_ref, o_ref, st_ref):
    L = GLA_CHUNK if q_ref.shape[0] % GLA_CHUNK == 0 else q_ref.shape[0]
    H, DK, DV = GLA_HEADS, GLA_DK, GLA_DV
    levels = lvl_ref.shape[0] // L - 1

    @pl.when(pl.program_id(1) == 0)
    def _():
        st_ref[...] = jnp.zeros_like(st_ref)

    row = lax.broadcasted_iota(jnp.int32, (L, L), 0)
    col = lax.broadcasted_iota(jnp.int32, (L, L), 1)
    diff = jnp.where(row > col, row ^ col, 0)

    for c in range(q_ref.shape[0] // L):
        rs = slice(c * L, (c + 1) * L)
        hi, mid, _ = _split3(la_ref[rs, :])
        args = _dot(lvl_ref[...], jnp.concatenate([hi, mid], axis=0))
        for h in range(H):
            ks = slice(h * DK, (h + 1) * DK)
            vs = slice(h * DV, (h + 1) * DV)
            q = q_ref[rs, ks].astype(F32)
            k = k_ref[rs, ks].astype(F32)
            v = v_ref[rs, vs]
            g = args[0:L, ks]
            a = jnp.where(row == col, jnp.sum(q * k, axis=-1, keepdims=True), 0.0)
            for j in range(levels):
                e = jnp.exp(args[(j + 1) * L:(j + 2) * L, ks])
                aj = _dot_nt((q * e).astype(BF16), (k * e).astype(BF16))
                a = jnp.where((diff >> j) == 1, aj, a)
            st = st_ref[h]
            o = _dot(a.astype(BF16), v) + _dot_nt((q * jnp.exp(g)).astype(BF16), st.astype(BF16))
            g_last = g[L - 1:L, :]
            kd = (k * jnp.exp(g_last - g)).astype(BF16)
            st_ref[h] = st * jnp.exp(g_last) + _dot_tn(v, kd)
            on = _rms(o, nw_ref[:, vs])
            o_ref[rs, vs] = (on * sr_ref[rs, vs].astype(F32)).astype(BF16)


def _gla_mixer(x, b, s, nw, w_all, jm, w_in, w_gate_up, b_gate, norm_w):
    m = b * s
    D, H, DKT, R = D_MODEL, GLA_HEADS, GLA_DK_TOTAL, GLA_RANK
    tm = min(GLA_ROW_TILE, m)
    wg = jnp.pad(w_in[:, 2 * DKT + 2 * D:].astype(BF16), ((0, 0), (0, LANES - R)))
    wup = jnp.pad(w_gate_up.astype(BF16), ((0, LANES - R), (0, 0)))

    row = pl.BlockSpec((tm, D), lambda i: (i, 0))
    half = pl.BlockSpec((tm, DKT), lambda i: (i, 0))
    q, k, v, sr, la = pl.pallas_call(
        _gla_proj_kernel,
        grid=(m // tm,),
        in_specs=[row, _resident((1, D)), _pick((D, 2 * DKT + 2 * D + R), jm), _resident((D, LANES)),
                  _resident((LANES, DKT)), _resident((1, DKT))],
        out_specs=[half, half, row, row, half],
        out_shape=[jax.ShapeDtypeStruct((m, DKT), BF16), jax.ShapeDtypeStruct((m, DKT), BF16),
                   jax.ShapeDtypeStruct((m, D), BF16), jax.ShapeDtypeStruct((m, D), BF16),
                   jax.ShapeDtypeStruct((m, DKT), F32)],
        compiler_params=_params(("parallel",)),
        name="gla_proj",
    )(x, nw, w_all, wg, wup, b_gate.reshape(1, DKT))

    L = min(GLA_CHUNK, s)
    levels = L.bit_length() - 1
    rows = min(GLA_CHUNKS_PER_STEP * L, s)
    nc = s // rows
    lvl = _gla_level_matrix(L, levels)
    lvl = jnp.asarray(np.concatenate([lvl, lvl], axis=1), dtype=BF16)
    kblk = pl.BlockSpec((rows, DKT), lambda i, j: (i * nc + j, 0))
    vblk = pl.BlockSpec((rows, D), lambda i, j: (i * nc + j, 0))
    og = pl.pallas_call(
        _gla_core_kernel,
        grid=(b, nc),
        in_specs=[kblk, kblk, vblk, vblk, kblk, _resident(((levels + 1) * L, 2 * L)), _resident((1, D))],
        out_specs=vblk,
        out_shape=jax.ShapeDtypeStruct((m, D), BF16),
        scratch_shapes=[pltpu.VMEM((H, GLA_DV, GLA_DK), F32)],
        compiler_params=_params(("parallel", "arbitrary")),
        name="gla_core",
    )(q, k, v, sr, la, lvl, norm_w.reshape(1, D))
    return og


def kernel(x, norm_w, ffn_w_in, ffn_w_out, mlstm_w_in, mlstm_b_gate, mlstm_conv_w, mlstm_norm_w, mlstm_w_out,
           fox_w_in, fox_b_f, fox_qk_norm, fox_w_out, gla_w_in, gla_w_gate_up, gla_b_gate, gla_norm_w,
           gla_w_out, final_norm_w):
    b, s, d = x.shape
    depth = norm_w.shape[0]
    xf = x.reshape(b * s, d)
    fw = final_norm_w.reshape(1, d)
    ffn_in, ffn_out = ffn_w_in.astype(BF16), ffn_w_out.astype(BF16)
    mlstm_in, mlstm_out = mlstm_w_in.astype(BF16), mlstm_w_out.astype(BF16)
    fox_in, fox_out = fox_w_in.astype(BF16), fox_w_out.astype(BF16)
    gla_in, gla_out = gla_w_in.astype(BF16), gla_w_out.astype(BF16)
    for layer in range(depth):
        kind, j = layer % N_MIXERS, layer // N_MIXERS
        xf = _ffn(xf, norm_w[layer, 0].reshape(1, d), ffn_in, ffn_out, (layer, 0), fw, False)
        nw = norm_w[layer, 1].reshape(1, d)
        if kind == 0:
            a = _mlstm_mixer(xf, b, s, nw, mlstm_in, j, mlstm_w_in[j], mlstm_b_gate[j], mlstm_conv_w[j],
                             mlstm_norm_w[j])
            mix = (a, mlstm_out, j)
        elif kind == 1:
            a = _fox_mixer(xf, b, s, nw, fox_in, j, fox_w_in[j], fox_b_f[j], fox_qk_norm[j])
            mix = (a, fox_out, j)
        else:
            a = _gla_mixer(xf, b, s, nw, gla_in, j, gla_w_in[j], gla_w_gate_up[j], gla_b_gate[j], gla_norm_w[j])
            mix = (a, gla_out, j)
        xf = _ffn(xf, norm_w[layer, 2].reshape(1, d), ffn_in, ffn_out, (layer, 1), fw, layer == depth - 1, mix)
    return xf.reshape(b, s, d)
```

```python
import functools

import numpy as np
import jax
import jax.numpy as jnp
from jax import lax
from jax.experimental import pallas as pl
from jax.experimental.pallas import tpu as pltpu

F32 = jnp.float32
BF16 = jnp.bfloat16

EPS = 1e-6
D_MODEL = 1024
D_FF = 2816
MLSTM_HEADS = 4
MLSTM_DH = D_MODEL // MLSTM_HEADS
CONV_WIDTH = 4
FOX_HEADS = 8
FOX_DH = D_MODEL // FOX_HEADS
GLA_HEADS = 4
GLA_DK_TOTAL = D_MODEL // 2
GLA_DK = GLA_DK_TOTAL // GLA_HEADS
GLA_DV = D_MODEL // GLA_HEADS
GLA_RANK = 16
GLA_TAU = 16.0
N_MIXERS = 3

LANES = 128
SUBLANES = 8
VMEM_LIMIT_BYTES = 56 * 1024 * 1024

ROW_TILE = 512
FFN_ROW_TILE = 1024
FFN_HIDDEN_PARTS = ((0, 1536), (1536, D_FF))
MLSTM_CHUNK = 256
MLSTM_CHUNKS_PER_STEP = 4
MLSTM_CONV_COLS = 512
MLSTM_ROW_TILE = 512
MLSTM_ROW_GROUPS = 2
GLA_CHUNK = 128
GLA_ROW_TILE = 1024
GLA_CHUNKS_PER_STEP = 8
FOX_ROW_TILE = 1024
FOX_ROW_GROUPS = 4
ATTN_TILE = 512
ATTN_HEADS_PER_STEP = 4
LOG2E = 1.4426950408889634


def _dot(a, b):
    return jnp.dot(a, b, preferred_element_type=F32)


def _dot_nt(a, b):
    return lax.dot_general(a, b, (((1,), (1,)), ((), ())), preferred_element_type=F32)


def _dot_tn(a, b):
    return lax.dot_general(a, b, (((0,), (0,)), ((), ())), preferred_element_type=F32)


def _rms(x, w):
    ms = jnp.mean(x * x, axis=-1, keepdims=True)
    return x * lax.rsqrt(ms + EPS) * w


def _sigmoid(x):
    return 1.0 / (1.0 + jnp.exp(-x))


def _log_sigmoid(x):
    return jnp.minimum(x, 0.0) - jnp.log(1.0 + jnp.exp(-jnp.abs(x)))


def _split3(x):
    hi = x.astype(BF16)
    r1 = x - hi.astype(F32)
    mid = r1.astype(BF16)
    lo = (r1 - mid.astype(F32)).astype(BF16)
    return hi, mid, lo


def _cumsum_rows(tril, x):
    hi, mid, lo = _split3(x)
    return _dot(tril, hi) + _dot(tril, mid) + _dot(tril, lo)


def _cumsum_lanes(x, triu):
    hi, mid, lo = _split3(x)
    return _dot(hi, triu) + _dot(mid, triu) + _dot(lo, triu)


def _tri(n):
    row = lax.broadcasted_iota(jnp.int32, (n, n), 0)
    col = lax.broadcasted_iota(jnp.int32, (n, n), 1)
    tril = jnp.where(row >= col, 1.0, 0.0).astype(BF16)
    triu = jnp.where(row <= col, 1.0, 0.0).astype(BF16)
    return row, col, tril, triu


def _resident(shape):
    zeros = (0,) * len(shape)
    return pl.BlockSpec(shape, lambda *_: zeros, pipeline_mode=pl.Buffered(1))


def _pick(shape, *lead):
    zeros = (0,) * len(shape)
    return pl.BlockSpec((None,) * len(lead) + tuple(shape), lambda *_: tuple(lead) + zeros,
                        pipeline_mode=pl.Buffered(1))


def _params(semantics):
    return pltpu.CompilerParams(dimension_semantics=semantics, vmem_limit_bytes=VMEM_LIMIT_BYTES)


def _row_tile(s):
    return min(ROW_TILE, s)


def _ffn_kernel(*refs, mixed, final):
    if mixed:
        x_ref, a_ref, wmix_ref, nw_ref, win_ref, wout_ref, fw_ref, o_ref = refs
        x = x_ref[...] + _dot(a_ref[...], wmix_ref[...])
    else:
        x_ref, nw_ref, win_ref, wout_ref, fw_ref, o_ref = refs
        x = x_ref[...]
    xn = _rms(x, nw_ref[...]).astype(BF16)
    acc = None
    for lo, hi in FFN_HIDDEN_PARTS:
        g = _dot(xn, win_ref[:, lo:hi])
        u = _dot(xn, win_ref[:, D_FF + lo:D_FF + hi])
        part = _dot((g * _sigmoid(g) * u).astype(BF16), wout_ref[lo:hi, :])
        acc = part if acc is None else acc + part
    y = x + 0.5 * acc
    if final:
        y = _rms(y, fw_ref[...])
    o_ref[...] = y


def _ffn(x, nw, w_in, w_out, sel, fw, final, mix=None):
    m = x.shape[0]
    tm = min(FFN_ROW_TILE, m)
    row = pl.BlockSpec((tm, D_MODEL), lambda i: (i, 0))
    weights = [_resident((1, D_MODEL)), _pick((D_MODEL, 2 * D_FF), *sel), _pick((D_FF, D_MODEL), *sel),
               _resident((1, D_MODEL))]
    if mix is None:
        in_specs, args = [row] + weights, (x, nw, w_in, w_out, fw)
    else:
        a, w_mix, j = mix
        in_specs = [row, row, _pick((D_MODEL, D_MODEL), j)] + weights
        args = (x, a, w_mix, nw, w_in, w_out, fw)
    return pl.pallas_call(
        functools.partial(_ffn_kernel, mixed=mix is not None, final=final),
        grid=(m // tm,),
        in_specs=in_specs,
        out_specs=row,
        out_shape=jax.ShapeDtypeStruct((m, D_MODEL), F32),
        compiler_params=_params(("parallel",)),
        name="ffn" + ("_mix" if mix is not None else "") + ("_final" if final else ""),
    )(*args)


def _mlstm_proj_kernel(x_ref, nw_ref, w_ref, wg_ref, wgt_ref, bcol_ref, brow_ref, cw_ref,
                       q_ref, k_ref, v_ref, so_ref, gcol_ref, grow_ref, carry_ref):
    tm = x_ref.shape[0]

    @pl.when(pl.program_id(1) == 0)
    def _():
        carry_ref[:SUBLANES, :] = jnp.zeros((SUBLANES, carry_ref.shape[1]), F32)

    rg = tm // MLSTM_ROW_GROUPS
    for r in range(MLSTM_ROW_GROUPS):
        rows = slice(r * rg, (r + 1) * rg)
        lo = SUBLANES + r * rg
        xn = _rms(x_ref[rows, :], nw_ref[...]).astype(BF16)
        for c in range(2 * D_MODEL // MLSTM_CONV_COLS):
            cs = slice(c * MLSTM_CONV_COLS, (c + 1) * MLSTM_CONV_COLS)
            carry_ref[lo:lo + rg, cs] = _dot(xn, w_ref[:, cs])
            cw = cw_ref[:, cs]
            conv = carry_ref[lo:lo + rg, cs] * cw[CONV_WIDTH - 1:CONV_WIDTH]
            for j in range(1, CONV_WIDTH):
                conv = conv + carry_ref[lo - j:lo - j + rg, cs] * cw[CONV_WIDTH - 1 - j:CONV_WIDTH - j]
            act = conv * _sigmoid(conv)
            if c * MLSTM_CONV_COLS < D_MODEL:
                q_ref[rows, cs] = (act * MLSTM_DH ** -0.5).astype(BF16)
            else:
                k_ref[rows, slice(cs.start - D_MODEL, cs.stop - D_MODEL)] = act.astype(BF16)
        v_ref[rows, :] = _dot(xn, w_ref[:, 2 * D_MODEL:3 * D_MODEL]).astype(BF16)
        so_ref[rows, :] = _sigmoid(_dot(xn, w_ref[:, 3 * D_MODEL:4 * D_MODEL])).astype(BF16)
        gcol_ref[rows, :] = _dot(xn, wg_ref[...]) + bcol_ref[...]
        grow_ref[:, rows] = _dot_nt(wgt_ref[...], xn) + brow_ref[...]
    carry_ref[:SUBLANES, :] = carry_ref[tm:, :]


def _select_lane(parts, sel):
    return _dot(parts[0], sel) + _dot(parts[1], sel) + _dot(parts[2], sel)


def _mlstm_core_kernel(q_ref, k_ref, v_ref, so_ref, gcol_ref, grow_ref, nw_ref, o_ref, cn_ref, m_ref):
    L = MLSTM_CHUNK if q_ref.shape[0] % MLSTM_CHUNK == 0 else q_ref.shape[0]

    @pl.when(pl.program_id(1) == 0)
    def _():
        cn_ref[...] = jnp.zeros_like(cn_ref)
        m_ref[...] = jnp.zeros_like(m_ref)

    _, _, tril, triu = _tri(L)
    row = lax.broadcasted_iota(jnp.int32, (L, LANES), 0)
    lane = lax.broadcasted_iota(jnp.int32, (L, LANES), 1)
    sel_row = lax.broadcasted_iota(jnp.int32, (LANES, LANES), 0)
    ones = jnp.ones((L, LANES), BF16)
    for c in range(q_ref.shape[0] // L):
        _mlstm_chunk(slice(c * L, (c + 1) * L), q_ref, k_ref, v_ref, so_ref, gcol_ref, grow_ref, nw_ref, o_ref,
                     cn_ref, m_ref, tril, triu, row, lane, sel_row, ones)


def _mlstm_chunk(rs, q_ref, k_ref, v_ref, so_ref, gcol_ref, grow_ref, nw_ref, o_ref, cn_ref, m_ref,
                 tril, triu, row, lane, sel_row, ones):
    L = rs.stop - rs.start
    H, DH = MLSTM_HEADS, MLSTM_DH
    nl = L // LANES
    gcol = gcol_ref[rs, :]
    grow = grow_ref[:, rs]
    bc_all = _cumsum_rows(tril, _log_sigmoid(gcol))
    br_all = _cumsum_lanes(_log_sigmoid(grow), triu)
    gcol_parts = _split3(gcol)
    bc_parts = _split3(bc_all)

    for h in range(H):
        hs = slice(h * DH, (h + 1) * DH)
        q = q_ref[rs, hs]
        k = k_ref[rs, hs]
        v_aug = jnp.concatenate([v_ref[rs, hs], ones], axis=1)
        icb = _select_lane(gcol_parts, jnp.where(sel_row == h, 1.0, 0.0).astype(BF16))
        bcb = _select_lane(bc_parts, jnp.where(sel_row == H + h, 1.0, 0.0).astype(BF16))
        a_row = grow[h:h + 1, :] - br_all[H + h:H + h + 1, :]
        m_prev = m_ref[h]
        cn_prev = cn_ref[h]

        a_tiles = [jnp.where(row >= lane + j * LANES, a_row[:, j * LANES:(j + 1) * LANES], -jnp.inf)
                   for j in range(nl)]
        a_max = a_tiles[0]
        for j in range(1, nl):
            a_max = jnp.maximum(a_max, a_tiles[j])
        u = jnp.maximum(m_prev, jnp.max(a_max, axis=-1, keepdims=True))
        inter = jnp.exp(m_prev - u)
        s = _dot_nt(q, k)
        sc = jnp.concatenate([s[:, j * LANES:(j + 1) * LANES] * jnp.exp(a_tiles[j] - u) for j in range(nl)],
                             axis=1)
        tot = _dot(sc.astype(BF16), v_aug) + jnp.concatenate([inter] * (DH // LANES + 1), axis=1) * _dot(
            q, cn_prev.astype(BF16))
        den = tot[:, DH:]
        r = 1.0 / jnp.maximum(jnp.abs(den), jnp.exp(-(bcb + u)))
        h_out = tot[:, :DH] * jnp.concatenate([r] * (DH // LANES), axis=1)

        b_last = bcb[L - 1:L, :]
        m_new = b_last + jnp.maximum(m_prev, jnp.max(a_row, axis=-1, keepdims=True))
        decay = jnp.exp(b_last + m_prev - m_new)
        w = jnp.exp(b_last - bcb + icb - m_new)
        kw = (k.astype(F32) * jnp.concatenate([w] * (DH // LANES), axis=1)).astype(BF16)
        cn_ref[h] = jnp.concatenate([decay] * (DH // LANES + 1), axis=1) * cn_prev + _dot_tn(kw, v_aug)
        m_ref[h] = m_new

        hn = _rms(h_out, nw_ref[:, hs])
        o_ref[rs, hs] = (hn * so_ref[rs, hs].astype(F32)).astype(BF16)


def _mlstm_mixer(x, b, s, nw, w_all, jm, w_in, b_gate, conv_w, norm_w):
    m = b * s
    D, H = D_MODEL, MLSTM_HEADS
    tm = min(MLSTM_ROW_TILE, s)
    nt = s // tm
    wgate = w_in[:, 4 * D:].astype(BF16)
    wg = jnp.pad(wgate, ((0, 0), (0, LANES - 2 * H)))
    wgt = wgate.T
    bcol = jnp.pad(b_gate, (0, LANES - 2 * H)).reshape(1, LANES)
    brow = b_gate.reshape(2 * H, 1)

    row = pl.BlockSpec((tm, D), lambda i, j: (i * nt + j, 0))
    q, k, v, so, gcol, grow = pl.pallas_call(
        _mlstm_proj_kernel,
        grid=(b, nt),
        in_specs=[row, _resident((1, D)), _pick((D, 4 * D + 2 * H), jm),
                  _resident((D, LANES)), _resident((2 * H, D)), _resident((1, LANES)), _resident((2 * H, 1)),
                  _resident((CONV_WIDTH, 2 * D))],
        out_specs=[row, row, row, row,
                   pl.BlockSpec((tm, LANES), lambda i, j: (i * nt + j, 0)),
                   pl.BlockSpec((2 * H, tm), lambda i, j: (0, i * nt + j))],
        out_shape=[jax.ShapeDtypeStruct((m, D), BF16)] * 4 + [
            jax.ShapeDtypeStruct((m, LANES), F32), jax.ShapeDtypeStruct((2 * H, m), F32)],
        scratch_shapes=[pltpu.VMEM((SUBLANES + tm, 2 * D), F32)],
        compiler_params=_params(("parallel", "arbitrary")),
        name="mlstm_proj",
    )(x, nw, w_all, wg, wgt, bcol, brow, conv_w)

    L = min(MLSTM_CHUNKS_PER_STEP * MLSTM_CHUNK, s)
    nc = s // L
    blk = pl.BlockSpec((L, D), lambda i, j: (i * nc + j, 0))
    hg = pl.pallas_call(
        _mlstm_core_kernel,
        grid=(b, nc),
        in_specs=[blk, blk, blk, blk,
                  pl.BlockSpec((L, LANES), lambda i, j: (i * nc + j, 0)),
                  pl.BlockSpec((2 * H, L), lambda i, j: (0, i * nc + j)),
                  _resident((1, D))],
        out_specs=blk,
        out_shape=jax.ShapeDtypeStruct((m, D), BF16),
        scratch_shapes=[pltpu.VMEM((H, MLSTM_DH, MLSTM_DH + LANES), F32), pltpu.VMEM((H, 1, LANES), F32)],
        compiler_params=_params(("parallel", "arbitrary")),
        name="mlstm_core",
    )(q, k, v, so, gcol, grow, norm_w.reshape(1, D))
    return hg


def _fox_proj_kernel(x_ref, nw_ref, w_ref, wf_ref, wft_ref, bcol_ref, brow_ref, qkn_ref,
                     q_ref, k_ref, v_ref, so_ref, fcol_ref, frow_ref, ccol_ref, crow_ref):
    tm = x_ref.shape[0]

    @pl.when(pl.program_id(1) == 0)
    def _():
        ccol_ref[...] = jnp.zeros_like(ccol_ref)
        crow_ref[...] = jnp.zeros_like(crow_ref)

    rg = tm // FOX_ROW_GROUPS
    _, _, tril, triu = _tri(rg)
    qkn = qkn_ref[...]
    for r in range(FOX_ROW_GROUPS):
        rows = slice(r * rg, (r + 1) * rg)
        xn = _rms(x_ref[rows, :], nw_ref[...]).astype(BF16)
        q = _dot(xn, w_ref[:, :D_MODEL])
        k = _dot(xn, w_ref[:, D_MODEL:2 * D_MODEL])
        for h in range(FOX_HEADS):
            hs = slice(h * FOX_DH, (h + 1) * FOX_DH)
            q_ref[rows, hs] = (_rms(q[:, hs], qkn[0:1]) * (FOX_DH ** -0.5 * LOG2E)).astype(BF16)
            k_ref[rows, hs] = _rms(k[:, hs], qkn[1:2]).astype(BF16)
        v_ref[rows, :] = _dot(xn, w_ref[:, 2 * D_MODEL:3 * D_MODEL]).astype(BF16)
        so_ref[rows, :] = _sigmoid(_dot(xn, w_ref[:, 3 * D_MODEL:4 * D_MODEL])).astype(BF16)

        lf_col = _log_sigmoid(_dot(xn, wf_ref[...]) + bcol_ref[...])
        lf_row = _log_sigmoid(_dot_nt(wft_ref[...], xn) + brow_ref[...])
        f_col = _cumsum_rows(tril, lf_col) + ccol_ref[...]
        f_row = _cumsum_lanes(lf_row, triu) + crow_ref[...]
        fcol_ref[rows, :] = f_col * LOG2E
        for h in range(FOX_HEADS):
            frow_ref[h, :, rows] = f_row[h:h + 1, :] * LOG2E
        ccol_ref[...] = f_col[rg - 1:rg, :]
        crow_ref[...] = f_row[:, rg - 1:rg]


def _fox_attn_kernel(q_ref, k_ref, v_ref, so_ref, fcol_ref, frow_ref, o_ref,
                     s_ref, m_ref, l_ref, acc_ref, fq_ref):
    t = q_ref.shape[0]
    G = s_ref.shape[0]
    nl = t // LANES
    hp = pl.program_id(1)
    qi = pl.program_id(2)

    lane = lax.broadcasted_iota(jnp.int32, (t, LANES), 1)
    row = lax.broadcasted_iota(jnp.int32, (t, LANES), 0)
    for g in range(G):
        fq = jnp.sum(jnp.where(lane == hp * G + g, fcol_ref[...], 0.0), axis=-1, keepdims=True)
        fq_ref[g] = jnp.broadcast_to(fq, (t, LANES))
    m_ref[...] = jnp.full_like(m_ref, -jnp.inf)
    l_ref[...] = jnp.zeros_like(l_ref)
    acc_ref[...] = jnp.zeros_like(acc_ref)

    def scores(kt, diagonal):
        off = pl.multiple_of(kt * t, t)
        for g in range(G):
            hs = slice(g * FOX_DH, (g + 1) * FOX_DH)
            s = _dot_nt(q_ref[:, hs], k_ref[pl.ds(off, t), hs])
            fk = frow_ref[g, :, pl.ds(off, t)]
            fqb = fq_ref[g]
            m = m_ref[g]
            for j in range(nl):
                ls = slice(j * LANES, (j + 1) * LANES)
                sj = s[:, ls] + (fqb - fk[:, ls])
                if diagonal:
                    sj = jnp.where(row >= lane + j * LANES, sj, -jnp.inf)
                s_ref[g, kt, :, ls] = sj
                m = jnp.maximum(m, sj)
            m_ref[g] = m

    def scores_body(kt, carry):
        scores(kt, False)
        return carry

    lax.fori_loop(0, qi, scores_body, 0)
    scores(qi, True)
    for g in range(G):
        m_ref[g] = jnp.broadcast_to(jnp.max(m_ref[g], axis=-1, keepdims=True), (t, LANES))

    def accumulate(kt, carry):
        off = pl.multiple_of(kt * t, t)
        for g in range(G):
            hs = slice(g * FOX_DH, (g + 1) * FOX_DH)
            mb = m_ref[g]
            l = l_ref[g]
            ps = []
            for j in range(nl):
                p = jnp.exp2(s_ref[g, kt, :, j * LANES:(j + 1) * LANES] - mb)
                l = l + p
                ps.append(p.astype(BF16))
            l_ref[g] = l
            acc_ref[g] += _dot(jnp.concatenate(ps, axis=1), v_ref[pl.ds(off, t), hs])
        return carry

    lax.fori_loop(0, qi + 1, accumulate, 0)
    for g in range(G):
        hs = slice(g * FOX_DH, (g + 1) * FOX_DH)
        l = jnp.sum(l_ref[g], axis=-1, keepdims=True)
        o_ref[:, hs] = (acc_ref[g] * (1.0 / l) * so_ref[:, hs].astype(F32)).astype(BF16)


def _fox_mixer(x, b, s, nw, w_all, jm, w_in, b_f, qk_norm):
    m = b * s
    D, H, DH = D_MODEL, FOX_HEADS, FOX_DH
    tm = min(FOX_ROW_TILE, s)
    nt = s // tm
    wgate = w_in[:, 4 * D:].astype(BF16)
    wf = jnp.pad(wgate, ((0, 0), (0, LANES - H)))
    wft = wgate.T
    bcol = jnp.pad(b_f, (0, LANES - H)).reshape(1, LANES)
    brow = b_f.reshape(H, 1)

    row = pl.BlockSpec((tm, D), lambda i, j: (i * nt + j, 0))
    q, k, v, so, fcol, frow = pl.pallas_call(
        _fox_proj_kernel,
        grid=(b, nt),
        in_specs=[row, _resident((1, D)), _pick((D, 4 * D + H), jm), _resident((D, LANES)), _resident((H, D)),
                  _resident((1, LANES)), _resident((H, 1)), _resident((2, DH))],
        out_specs=[row, row, row, row,
                   pl.BlockSpec((tm, LANES), lambda i, j: (i * nt + j, 0)),
                   pl.BlockSpec((H, 1, tm), lambda i, j: (0, 0, i * nt + j))],
        out_shape=[jax.ShapeDtypeStruct((m, D), BF16)] * 4 + [
            jax.ShapeDtypeStruct((m, LANES), F32), jax.ShapeDtypeStruct((H, 1, m), F32)],
        scratch_shapes=[pltpu.VMEM((1, LANES), F32), pltpu.VMEM((H, 1), F32)],
        compiler_params=_params(("parallel", "arbitrary")),
        name="fox_proj",
    )(x, nw, w_all, wf, wft, bcol, brow, qk_norm)

    t = min(ATTN_TILE, s)
    nq = s // t
    G = ATTN_HEADS_PER_STEP
    qblk = pl.BlockSpec((t, G * DH), lambda bi, hi, qi: (bi * nq + qi, hi))
    kblk = pl.BlockSpec((s, G * DH), lambda bi, hi, qi: (bi, hi))
    att = pl.pallas_call(
        _fox_attn_kernel,
        grid=(b, H // G, nq),
        in_specs=[qblk, kblk, kblk, qblk,
                  pl.BlockSpec((t, LANES), lambda bi, hi, qi: (bi * nq + qi, 0)),
                  pl.BlockSpec((G, 1, s), lambda bi, hi, qi: (hi, 0, bi))],
        out_specs=qblk,
        out_shape=jax.ShapeDtypeStruct((m, D), BF16),
        scratch_shapes=[pltpu.VMEM((G, nq, t, t), F32), pltpu.VMEM((G, t, LANES), F32),
                        pltpu.VMEM((G, t, LANES), F32), pltpu.VMEM((G, t, DH), F32),
                        pltpu.VMEM((G, t, LANES), F32)],
        compiler_params=_params(("parallel", "parallel", "arbitrary")),
        name="fox_attn",
    )(q, k, v, so, fcol, frow)
    return att


def _gla_level_matrix(L, levels):
    t = np.arange(L)[:, None]
    c = np.arange(L)[None, :]
    blocks = [(c <= t)]
    for j in range(levels):
        hs = 1 << j
        mid = (t & ~(2 * hs - 1)) + hs
        upper = (t & hs) != 0
        blocks.append(np.where(upper, (c > mid) & (c <= t), (c > t) & (c <= mid)))
    return np.concatenate(blocks, axis=0).astype(np.float32)


def _gla_proj_kernel(x_ref, nw_ref, w_ref, wg_ref, wup_ref, bg_ref,
                     q_ref, k_ref, v_ref, sr_ref, la_ref):
    xn = _rms(x_ref[...], nw_ref[...]).astype(BF16)
    DKT = GLA_DK_TOTAL
    q_ref[...] = (_dot(xn, w_ref[:, :DKT]) * GLA_DK ** -0.5).astype(BF16)
    k_ref[...] = _dot(xn, w_ref[:, DKT:2 * DKT]).astype(BF16)
    v_ref[...] = _dot(xn, w_ref[:, 2 * DKT:2 * DKT + D_MODEL]).astype(BF16)
    r = _dot(xn, w_ref[:, 2 * DKT + D_MODEL:2 * DKT + 2 * D_MODEL])
    sr_ref[...] = (r * _sigmoid(r)).astype(BF16)
    glr = _dot(xn, wg_ref[...])
    z = _dot(glr.astype(BF16), wup_ref[...]) + bg_ref[...]
    la_ref[...] = _log_sigmoid(z) * (1.0 / GLA_TAU)


def _gla_core_kernel(q_ref, k_ref, v_ref, sr_ref, la_ref, lvl_ref, nw_ref, o_ref, st_ref):
    L = GLA_CHUNK if q_ref.shape[0] % GLA_CHUNK == 0 else q_ref.shape[0]
    H, DK, DV = GLA_HEADS, GLA_DK, GLA_DV
    levels = lvl_ref.shape[0] // L - 1

    @pl.when(pl.program_id(1) == 0)
    def _():
        st_ref[...] = jnp.zeros_like(st_ref)

    row = lax.broadcasted_iota(jnp.int32, (L, L), 0)
    col = lax.broadcasted_iota(jnp.int32, (L, L), 1)
    diff = jnp.where(row > col, row ^ col, 0)

    for c in range(q_ref.shape[0] // L):
        rs = slice(c * L, (c + 1) * L)
        hi, mid, _ = _split3(la_ref[rs, :])
        args = _dot(lvl_ref[...], jnp.concatenate([hi, mid], axis=0))
        for h in range(H):
            ks = slice(h * DK, (h + 1) * DK)
            vs = slice(h * DV, (h + 1) * DV)
            q = q_ref[rs, ks].astype(F32)
            k = k_ref[rs, ks].astype(F32)
            v = v_ref[rs, vs]
            g = args[0:L, ks]
            a = jnp.where(row == col, jnp.sum(q * k, axis=-1, keepdims=True), 0.0)
            for j in range(levels):
                e = jnp.exp(args[(j + 1) * L:(j + 2) * L, ks])
                aj = _dot_nt((q * e).astype(BF16), (k * e).astype(BF16))
                a = jnp.where((diff >> j) == 1, aj, a)
            st = st_ref[h]
            o = _dot(a.astype(BF16), v) + _dot_nt((q * jnp.exp(g)).astype(BF16), st.astype(BF16))
            g_last = g[L - 1:L, :]
            kd = (k * jnp.exp(g_last - g)).astype(BF16)
            st_ref[h] = st * jnp.exp(g_last) + _dot_tn(v, kd)
            on = _rms(o, nw_ref[:, vs])
            o_ref[rs, vs] = (on * sr_ref[rs, vs].astype(F32)).astype(BF16)


def _gla_mixer(x, b, s, nw, w_all, jm, w_in, w_gate_up, b_gate, norm_w):
    m = b * s
    D, H, DKT, R = D_MODEL, GLA_HEADS, GLA_DK_TOTAL, GLA_RANK
    tm = min(GLA_ROW_TILE, m)
    wg = jnp.pad(w_in[:, 2 * DKT + 2 * D:].astype(BF16), ((0, 0), (0, LANES - R)))
    wup = jnp.pad(w_gate_up.astype(BF16), ((0, LANES - R), (0, 0)))

    row = pl.BlockSpec((tm, D), lambda i: (i, 0))
    half = pl.BlockSpec((tm, DKT), lambda i: (i, 0))
    q, k, v, sr, la = pl.pallas_call(
        _gla_proj_kernel,
        grid=(m // tm,),
        in_specs=[row, _resident((1, D)), _pick((D, 2 * DKT + 2 * D + R), jm), _resident((D, LANES)),
                  _resident((LANES, DKT)), _resident((1, DKT))],
        out_specs=[half, half, row, row, half],
        out_shape=[jax.ShapeDtypeStruct((m, DKT), BF16), jax.ShapeDtypeStruct((m, DKT), BF16),
                   jax.ShapeDtypeStruct((m, D), BF16), jax.ShapeDtypeStruct((m, D), BF16),
                   jax.ShapeDtypeStruct((m, DKT), F32)],
        compiler_params=_params(("parallel",)),
        name="gla_proj",
    )(x, nw, w_all, wg, wup, b_gate.reshape(1, DKT))

    L = min(GLA_CHUNK, s)
    levels = L.bit_length() - 1
    rows = min(GLA_CHUNKS_PER_STEP * L, s)
    nc = s // rows
    lvl = _gla_level_matrix(L, levels)
    lvl = jnp.asarray(np.concatenate([lvl, lvl], axis=1), dtype=BF16)
    kblk = pl.BlockSpec((rows, DKT), lambda i, j: (i * nc + j, 0))
    vblk = pl.BlockSpec((rows, D), lambda i, j: (i * nc + j, 0))
    og = pl.pallas_call(
        _gla_core_kernel,
        grid=(b, nc),
        in_specs=[kblk, kblk, vblk, vblk, kblk, _resident(((levels + 1) * L, 2 * L)), _resident((1, D))],
        out_specs=vblk,
        out_shape=jax.ShapeDtypeStruct((m, D), BF16),
        scratch_shapes=[pltpu.VMEM((H, GLA_DV, GLA_DK), F32)],
        compiler_params=_params(("parallel", "arbitrary")),
        name="gla_core",
    )(q, k, v, sr, la, lvl, norm_w.reshape(1, D))
    return og


def kernel(x, norm_w, ffn_w_in, ffn_w_out, mlstm_w_in, mlstm_b_gate, mlstm_conv_w, mlstm_norm_w, mlstm_w_out,
           fox_w_in, fox_b_f, fox_qk_norm, fox_w_out, gla_w_in, gla_w_gate_up, gla_b_gate, gla_norm_w,
           gla_w_out, final_norm_w):
    b, s, d = x.shape
    depth = norm_w.shape[0]
    xf = x.reshape(b * s, d)
    fw = final_norm_w.reshape(1, d)
    ffn_in, ffn_out = ffn_w_in.astype(BF16), ffn_w_out.astype(BF16)
    mlstm_in, mlstm_out = mlstm_w_in.astype(BF16), mlstm_w_out.astype(BF16)
    fox_in, fox_out = fox_w_in.astype(BF16), fox_w_out.astype(BF16)
    gla_in, gla_out = gla_w_in.astype(BF16), gla_w_out.astype(BF16)
    for layer in range(depth):
        kind, j = layer % N_MIXERS, layer // N_MIXERS
        xf = _ffn(xf, norm_w[layer, 0].reshape(1, d), ffn_in, ffn_out, (layer, 0), fw, False)
        nw = norm_w[layer, 1].reshape(1, d)
        if kind == 0:
            a = _mlstm_mixer(xf, b, s, nw, mlstm_in, j, mlstm_w_in[j], mlstm_b_gate[j], mlstm_conv_w[j],
                             mlstm_norm_w[j])
            mix = (a, mlstm_out, j)
        elif kind == 1:
            a = _fox_mixer(xf, b, s, nw, fox_in, j, fox_w_in[j], fox_b_f[j], fox_qk_norm[j])
            mix = (a, fox_out, j)
        else:
            a = _gla_mixer(xf, b, s, nw, gla_in, j, gla_w_in[j], gla_w_gate_up[j], gla_b_gate[j], gla_norm_w[j])
            mix = (a, gla_out, j)
        xf = _ffn(xf, norm_w[layer, 2].reshape(1, d), ffn_in, ffn_out, (layer, 1), fw, layer == depth - 1, mix)
    return xf.reshape(b, s, d)
```
